```python
import math
import jax, jax.numpy as jnp
from jax import lax
import numpy as np


D_MODEL = 1024
BATCH = 4
SEQ = 4096
DEPTH = 2

N_MEM = 256
D_MIX = D_MODEL
D_CONV = D_MIX // 4
D_ATTN = D_MIX // 2
D_SSM = D_MIX // 4
ATTN_HEAD_DIM = 64
ATTN_HEADS = D_ATTN // ATTN_HEAD_DIM
SB_BLOCK = 128
CONV_WIDTH = 31
SSM_GROUP = 16
SSM_GROUPS = D_SSM // SSM_GROUP
SSM_STATE = 64
MEM_HEADS = 4
MEM_HEAD_DIM = D_MODEL // MEM_HEADS
PEER_HEADS = 8
PEER_KEYS = 128
PEER_EXPERTS = PEER_KEYS * PEER_KEYS
PEER_QDIM = 256
PEER_HALF = PEER_QDIM // 2
PEER_TOPK = 16
PEER_CHUNK = 128
D_IN = 2 * D_CONV + 3 * D_ATTN + D_SSM
EPS = 1e-6

kernel_name = 'hybrid_conv_sba_s5_peer_block'


def rmsnorm(x, g):
    x32 = x.astype(jnp.float32)
    y = x32 * lax.rsqrt(jnp.mean(x32 * x32, axis=-1, keepdims=True) + EPS)
    return (y * g.astype(jnp.float32)).astype(x.dtype)


def conformer_conv(u, conv_w, conv_b, ln_g, ln_b):
    val, gate = jnp.split(u, 2, axis=-1)
    y = val * jax.nn.sigmoid(gate)
    y = lax.conv_general_dilated(
        y, conv_w[:, None, :].astype(y.dtype), window_strides=(1,),
        padding=[(CONV_WIDTH - 1, 0)], dimension_numbers=('NWC', 'WIO', 'NWC'),
        feature_group_count=D_CONV) + conv_b
    y32 = y.astype(jnp.float32)
    mu = jnp.mean(y32, axis=-1, keepdims=True)
    var = jnp.mean(jnp.square(y32 - mu), axis=-1, keepdims=True)
    y32 = (y32 - mu) * lax.rsqrt(var + EPS) * ln_g.astype(jnp.float32) + ln_b.astype(jnp.float32)
    return jax.nn.silu(y32).astype(u.dtype)


def stick_breaking_attention(q, k, v):
    b, h, s, dh = q.shape
    n_blk = s // SB_BLOCK
    k32 = k.astype(jnp.float32)
    scale = 1.0 / math.sqrt(dh)
    key_pos = jnp.arange(s)

    def one_block(i):
        start = i * SB_BLOCK
        q_blk = lax.dynamic_slice_in_dim(q, start, SB_BLOCK, axis=2).astype(jnp.float32)
        z = jnp.einsum('bhqd,bhkd->bhqk', q_blk, k32) * scale
        q_pos = start + jnp.arange(SB_BLOCK)
        mask = key_pos[None, :] < q_pos[:, None]
        log_stay = jnp.where(mask, jax.nn.log_sigmoid(-z), 0.0)
        later = lax.cumsum(log_stay, axis=3, reverse=True) - log_stay
        a = jnp.where(mask, jnp.exp(jax.nn.log_sigmoid(z) + later), 0.0)
        return jnp.einsum('bhqk,bhkd->bhqd', a.astype(v.dtype), v)

    out = lax.map(one_block, jnp.arange(n_blk))
    return jnp.transpose(out, (1, 0, 3, 2, 4)).reshape(b, s, h * dh)


def _ssm_combine(c1, c2):
    a1, b1 = c1
    a2, b2 = c2
    return a1 * a2, a2 * b1 + b2


def s5_ssm(u, lam_re, lam_im, log_dt, b_re, b_im, c_re, c_im, d_skip, glu_w, glu_b):
    bsz, s, _ = u.shape
    f32 = jnp.float32
    u32 = u.astype(f32)
    lam = lax.complex(jnp.minimum(lam_re.astype(f32), -1e-4), lam_im.astype(f32))
    dt = jnp.exp(log_dt.astype(f32))[:, None]
    lam_bar = jnp.exp(lam * dt)
    b_mat = lax.complex(b_re.astype(f32), b_im.astype(f32))
    b_bar = ((lam_bar - 1.0) / lam)[..., None] * b_mat
    c_mat = lax.complex(c_re.astype(f32), c_im.astype(f32))
    ug = u32.reshape(bsz, s, SSM_GROUPS, SSM_GROUP).astype(jnp.complex64)
    bu = jnp.einsum('bsgh,gph->sbgp', ug, b_bar)
    a = jnp.broadcast_to(lam_bar[None, None], (s, 1, SSM_GROUPS, SSM_STATE))
    _, states = lax.associative_scan(_ssm_combine, (a, bu), axis=0)
    y = jnp.einsum('sbgp,ghp->bsgh', states, c_mat).real.reshape(bsz, s, D_SSM)
    y = y + d_skip.astype(f32) * u32
    y = jax.nn.gelu(y)
    y = y * jax.nn.sigmoid(y @ glu_w.astype(f32) + glu_b.astype(f32))
    return y.astype(u.dtype)


def memory_cross_attention(h, mem_h, w_q, w_kv, w_o):
    b, s, _ = h.shape
    m = mem_h.shape[1]
    q = (h @ w_q).reshape(b, s, MEM_HEADS, MEM_HEAD_DIM)
    k, v = jnp.split(mem_h @ w_kv, 2, axis=-1)
    k = k.reshape(b, m, MEM_HEADS, MEM_HEAD_DIM)
    v = v.reshape(b, m, MEM_HEADS, MEM_HEAD_DIM)
    scores = jnp.einsum('bshd,bmhd->bhsm', q.astype(jnp.float32), k.astype(jnp.float32))
    p = jax.nn.softmax(scores / math.sqrt(MEM_HEAD_DIM), axis=-1)
    o = jnp.einsum('bhsm,bmhd->bshd', p.astype(v.dtype), v).reshape(b, s, D_MODEL)
    return o @ w_o


def peer_ffn(h, w_q, sub_keys, u_tab, v_tab):
    b, s, d = h.shape
    tokens = h.reshape(-1, PEER_CHUNK, d)

    def one_chunk(xc):
        q = (xc @ w_q).reshape(PEER_CHUNK, PEER_HEADS, 2, PEER_HALF)
        sc = jnp.einsum('chid,hikd->chik', q.astype(jnp.float32), sub_keys.astype(jnp.float32))
        top_s, top_i = lax.top_k(sc, PEER_TOPK)
        cand = top_s[:, :, 0, :, None] + top_s[:, :, 1, None, :]
        cand = cand.reshape(PEER_CHUNK, PEER_HEADS, PEER_TOPK * PEER_TOPK)
        best_s, best_c = lax.top_k(cand, PEER_TOPK)
        i1 = jnp.take_along_axis(top_i[:, :, 0], best_c // PEER_TOPK, axis=-1)
        i2 = jnp.take_along_axis(top_i[:, :, 1], best_c % PEER_TOPK, axis=-1)
        expert = i1 * PEER_KEYS + i2
        gate = jax.nn.softmax(best_s, axis=-1)
        u_e = jnp.take(u_tab, expert, axis=0)
        v_e = jnp.take(v_tab, expert, axis=0)
        act = jax.nn.gelu(jnp.einsum('chkd,cd->chk', u_e, xc).astype(jnp.float32))
        return jnp.einsum('chk,chkd->cd', (gate * act).astype(v_e.dtype), v_e)

    return lax.map(one_chunk, tokens).reshape(b, s, d)


def setup_inputs(seed: int = 0) -> dict:
    key = jax.random.key(seed)
    ks = iter(jax.random.split(key, 48))
    f32 = jnp.float32

    def nrm(shape, scale):
        return jax.random.normal(next(ks), shape, f32) * scale

    def gain(shape):
        return 1.0 + nrm(shape, 0.02)

    L = DEPTH
    G, P, H = SSM_GROUPS, SSM_STATE, SSM_GROUP
    return {
        'x': nrm((BATCH, SEQ, D_MODEL), 1.0),
        'mem': nrm((BATCH, N_MEM, D_MODEL), 1.0),
        'norm_mix_g': gain((L, D_MODEL)),
        'w_in': nrm((L, D_MODEL, D_IN), D_MODEL ** -0.5),
        'conv_w': nrm((L, CONV_WIDTH, D_CONV), CONV_WIDTH ** -0.5),
        'conv_b': nrm((L, D_CONV), 0.02),
        'conv_ln_g': gain((L, D_CONV)),
        'conv_ln_b': nrm((L, D_CONV), 0.02),
        'ssm_lambda_re': -0.5 + nrm((L, G, P), 0.01),
        'ssm_lambda_im': math.pi * jnp.arange(P, dtype=f32) + nrm((L, G, P), 0.01),
        'ssm_log_dt': jax.random.uniform(next(ks), (L, G), f32, math.log(1e-3), math.log(1e-1)),
        'ssm_b_re': nrm((L, G, P, H), (2.0 * H) ** -0.5),
        'ssm_b_im': nrm((L, G, P, H), (2.0 * H) ** -0.5),
        'ssm_c_re': nrm((L, G, H, P), (2.0 * P) ** -0.5),
        'ssm_c_im': nrm((L, G, H, P), (2.0 * P) ** -0.5),
        'ssm_d': nrm((L, D_SSM), 1.0),
        'ssm_glu_w': nrm((L, D_SSM, D_SSM), D_SSM ** -0.5),
        'ssm_glu_b': nrm((L, D_SSM), 0.02),
        'grp_norm_g': gain((L, D_MIX)),
        'w_out': nrm((L, D_MIX, D_MODEL), D_MIX ** -0.5),
        'norm_mem_g': gain((L, D_MODEL)),
        'mem_norm_g': gain((L, D_MODEL)),
        'w_mq': nrm((L, D_MODEL, D_MODEL), D_MODEL ** -0.5),
        'w_mkv': nrm((L, D_MODEL, 2 * D_MODEL), D_MODEL ** -0.5),
        'w_mo': nrm((L, D_MODEL, D_MODEL), D_MODEL ** -0.5),
        'norm_ffn_g': gain((L, D_MODEL)),
        'peer_wq': nrm((L, D_MODEL, PEER_HEADS * PEER_QDIM), D_MODEL ** -0.5),
        'peer_sub_keys': nrm((L, PEER_HEADS, 2, PEER_KEYS, PEER_HALF), PEER_HALF ** -0.5),
        'peer_u': nrm((L, PEER_EXPERTS, D_MODEL), D_MODEL ** -0.5),
        'peer_v': nrm((L, PEER_EXPERTS, D_MODEL), 0.5),
        'final_norm_g': gain((D_MODEL,)),
    }


def reference(x, mem, norm_mix_g, w_in, conv_w, conv_b, conv_ln_g, conv_ln_b,
              ssm_lambda_re, ssm_lambda_im, ssm_log_dt, ssm_b_re, ssm_b_im,
              ssm_c_re, ssm_c_im, ssm_d, ssm_glu_w, ssm_glu_b, grp_norm_g, w_out,
              norm_mem_g, mem_norm_g, w_mq, w_mkv, w_mo, norm_ffn_g, peer_wq,
              peer_sub_keys, peer_u, peer_v, final_norm_g):
    b, s, _ = x.shape
    for l in range(DEPTH):
        h = rmsnorm(x, norm_mix_g[l])
        proj = h @ w_in[l]
        p_conv, p_attn, p_ssm = jnp.split(proj, [2 * D_CONV, 2 * D_CONV + 3 * D_ATTN], axis=-1)
        y_conv = conformer_conv(p_conv, conv_w[l], conv_b[l], conv_ln_g[l], conv_ln_b[l])
        q, k, v = jnp.split(p_attn, 3, axis=-1)
        q = q.reshape(b, s, ATTN_HEADS, ATTN_HEAD_DIM).transpose(0, 2, 1, 3)
        k = k.reshape(b, s, ATTN_HEADS, ATTN_HEAD_DIM).transpose(0, 2, 1, 3)
        v = v.reshape(b, s, ATTN_HEADS, ATTN_HEAD_DIM).transpose(0, 2, 1, 3)
        y_attn = stick_breaking_attention(q, k, v)
        y_ssm = s5_ssm(p_ssm, ssm_lambda_re[l], ssm_lambda_im[l], ssm_log_dt[l],
                       ssm_b_re[l], ssm_b_im[l], ssm_c_re[l], ssm_c_im[l], ssm_d[l],
                       ssm_glu_w[l], ssm_glu_b[l])
        g = grp_norm_g[l]
        y = jnp.concatenate([
            rmsnorm(y_conv, g[:D_CONV]),
            rmsnorm(y_attn, g[D_CONV:D_CONV + D_ATTN]),
            rmsnorm(y_ssm, g[D_CONV + D_ATTN:]),
        ], axis=-1)
        x = x + y @ w_out[l]
        x = x + memory_cross_attention(rmsnorm(x, norm_mem_g[l]), rmsnorm(mem, mem_norm_g[l]),
                                       w_mq[l], w_mkv[l], w_mo[l])
        x = x + peer_ffn(rmsnorm(x, norm_ffn_g[l]), peer_wq[l], peer_sub_keys[l],
                         peer_u[l], peer_v[l])
    return rmsnorm(x, final_norm_g)
```

```python
import functools
import math

import jax
import jax.numpy as jnp
from jax import lax
from jax.experimental import pallas as pl
from jax.experimental.pallas import tpu as pltpu

F32 = jnp.float32
BF16 = jnp.bfloat16
EPS = 1e-6

CONV_WIDTH = 31
CONV_HALO = 32
ATTN_HEAD_DIM = 64
SSM_GROUP = 16
SSM_STATE = 64
MEM_HEADS = 4
PEER_HEADS = 8
PEER_KEYS = 128
PEER_TOPK = 16
RANK_NONE = 99.0
VMEM_LIMIT = 50 * 1024 * 1024


def _params(sem, vmem=VMEM_LIMIT):
    return pltpu.CompilerParams(dimension_semantics=sem, vmem_limit_bytes=vmem)


def _rms(x, g):
    ms = jnp.mean(x * x, axis=-1, keepdims=True)
    return x * lax.rsqrt(ms + EPS) * g


def _gelu(x):
    c = math.sqrt(2.0 / math.pi)
    return 0.5 * x * (1.0 + jnp.tanh(c * (x + 0.044715 * (x * x * x))))


def _sigmoid(x):
    return 1.0 / (1.0 + jnp.exp(-x))


def _norm_matmul_kernel(x_ref, g_ref, w_ref, o_ref):
    h = _rms(x_ref[...], g_ref[...]).astype(BF16)
    o_ref[...] = jnp.dot(h, w_ref[...], preferred_element_type=F32)


def norm_matmul(x, g, w, tm):
    m, d = x.shape
    n = w.shape[1]
    return pl.pallas_call(
        _norm_matmul_kernel,
        grid=(m // tm,),
        in_specs=[pl.BlockSpec((tm, d), lambda i: (i, 0)),
                  pl.BlockSpec((1, d), lambda i: (0, 0)),
                  pl.BlockSpec((d, n), lambda i: (0, 0))],
        out_specs=pl.BlockSpec((tm, n), lambda i: (i, 0)),
        out_shape=jax.ShapeDtypeStruct((m, n), F32),
        compiler_params=_params(("parallel",)),
    )(x, g.reshape(1, d), w)


def _conv_kernel(cur_ref, halo_ref, w_ref, b_ref, lng_ref, lnb_ref, o_ref, ext_ref, *, ts, tiles_per_seq, dc):
    i = pl.program_id(0)
    cur = cur_ref[...]
    halo = halo_ref[...]
    y = cur[:, :dc] * _sigmoid(cur[:, dc:])
    yh = halo[:, :dc] * _sigmoid(halo[:, dc:])
    yh = jnp.where(i % tiles_per_seq == 0, 0.0, yh)
    ext_ref[0:CONV_HALO, :] = yh
    ext_ref[CONV_HALO:, :] = y
    acc = jnp.broadcast_to(b_ref[...], (ts, dc))
    for k in range(CONV_WIDTH):
        off = CONV_HALO - (CONV_WIDTH - 1) + k
        acc = acc + w_ref[k:k + 1, :] * ext_ref[off:off + ts, :]
    mu = jnp.mean(acc, axis=-1, keepdims=True)
    cen = acc - mu
    var = jnp.mean(cen * cen, axis=-1, keepdims=True)
    z = cen * lax.rsqrt(var + EPS) * lng_ref[...] + lnb_ref[...]
    o_ref[...] = z * _sigmoid(z)


def conformer_conv(proj, conv_w, conv_b, ln_g, ln_b, seq, ts):
    m = proj.shape[0]
    dc = conv_w.shape[1]
    kern = functools.partial(_conv_kernel, ts=ts, tiles_per_seq=seq // ts, dc=dc)
    halo_blocks = ts // CONV_HALO
    return pl.pallas_call(
        kern,
        grid=(m // ts,),
        in_specs=[pl.BlockSpec((ts, 2 * dc), lambda i: (i, 0)),
                  pl.BlockSpec((CONV_HALO, 2 * dc), lambda i: (jnp.maximum(i * halo_blocks - 1, 0), 0)),
                  pl.BlockSpec((CONV_WIDTH, dc), lambda i: (0, 0)),
                  pl.BlockSpec((1, dc), lambda i: (0, 0)),
                  pl.BlockSpec((1, dc), lambda i: (0, 0)),
                  pl.BlockSpec((1, dc), lambda i: (0, 0))],
        out_specs=pl.BlockSpec((ts, dc), lambda i: (i, 0)),
        out_shape=jax.ShapeDtypeStruct((m, dc), F32),
        scratch_shapes=[pltpu.VMEM((ts + CONV_HALO, dc), F32)],
        compiler_params=_params(("parallel",)),
    )(proj, proj, conv_w, conv_b.reshape(1, dc), ln_g.reshape(1, dc), ln_b.reshape(1, dc))


def _sba_kernel(q_ref, k_ref, v_ref, o_ref, *, tq, scale):
    i = pl.program_id(2)
    dh = ATTN_HEAD_DIM
    row = lax.broadcasted_iota(jnp.int32, (tq, tq), 0)
    col = lax.broadcasted_iota(jnp.int32, (tq, tq), 1)
    causal = col < row
    suffix = (row > col).astype(BF16)

    for hh in range(q_ref.shape[1] // dh):
        lanes = slice(hh * dh, (hh + 1) * dh)
        q = (q_ref[:, lanes] * scale).astype(BF16)

        def block(j, acc, run, diagonal):
            start = pl.multiple_of(j * tq, tq)
            kj = k_ref[pl.ds(start, tq), lanes].astype(BF16)
            vj = v_ref[pl.ds(start, tq), lanes].astype(BF16)
            z = lax.dot_general(q, kj, (((1,), (1,)), ((), ())), preferred_element_type=F32)
            sp = jnp.log(1.0 + jnp.exp(-jnp.abs(z)))
            log_beta = jnp.minimum(z, 0.0) - sp
            log_stay = -jnp.maximum(z, 0.0) - sp
            if diagonal:
                log_stay = jnp.where(causal, log_stay, 0.0)
            hi = log_stay.astype(BF16)
            lo = (log_stay - hi.astype(F32)).astype(BF16)
            later = (jnp.dot(hi, suffix, preferred_element_type=F32)
                     + jnp.dot(lo, suffix, preferred_element_type=F32))
            a = jnp.exp(log_beta + later + run)
            if diagonal:
                a = jnp.where(causal, a, 0.0)
            acc = acc + jnp.dot(a.astype(BF16), vj, preferred_element_type=F32)
            run = run + later[:, 0:1] + log_stay[:, 0:1]
            return acc, run

        acc0 = jnp.zeros((tq, dh), F32)
        run0 = jnp.zeros((tq, 1), F32)
        acc, run = block(i, acc0, run0, True)

        def body(jj, carry):
            return block(i - 1 - jj, carry[0], carry[1], False)

        acc, run = lax.fori_loop(0, i, body, (acc, run))
        o_ref[:, lanes] = acc


def stick_breaking_attention(proj, batch, seq, col0, d_attn, tq):
    lane_blk = 128
    hp = d_attn // lane_blk
    qb, kb, vb = col0 // lane_blk, (col0 + d_attn) // lane_blk, (col0 + 2 * d_attn) // lane_blk
    nq = seq // tq
    kern = functools.partial(_sba_kernel, tq=tq, scale=1.0 / math.sqrt(ATTN_HEAD_DIM))
    return pl.pallas_call(
        kern,
        grid=(batch, hp, nq),
        in_specs=[pl.BlockSpec((tq, lane_blk), lambda b, h, i: (b * nq + i, qb + h)),
                  pl.BlockSpec((seq, lane_blk), lambda b, h, i: (b, kb + h)),
                  pl.BlockSpec((seq, lane_blk), lambda b, h, i: (b, vb + h))],
        out_specs=pl.BlockSpec((tq, lane_blk), lambda b, h, i: (b * nq + i, h)),
        out_shape=jax.ShapeDtypeStruct((batch * seq, d_attn), F32),
        compiler_params=_params(("parallel", "parallel", "arbitrary")),
    )(proj, proj, proj)


def _ssm_kernel(u_ref, bblk_ref, pw_ref, cblk_ref, d_ref, gw_ref, gb_ref, o_ref, st_ref, x_ref, *, ts, n):
    c = pl.program_id(1)

    @pl.when(c == 0)
    def _():
        st_ref[...] = jnp.zeros_like(st_ref)

    u = u_ref[...]
    x_ref[...] = jnp.dot(u.astype(BF16), bblk_ref[...], preferred_element_type=F32)

    def group(r, carry):
        sr, si = carry
        rows = pl.ds(pl.multiple_of(r * 8, 8), 8)
        re = x_ref[rows, 0:n]
        im = x_ref[rows, n:2 * n]
        for lvl, d in enumerate((1, 2, 4)):
            lr = pw_ref[lvl, :, 0:n]
            li = pw_ref[lvl, :, n:2 * n]
            pr = pltpu.roll(re, d, 0)
            pi = pltpu.roll(im, d, 0)
            re, im = re + pr * lr - pi * li, im + pr * li + pi * lr
        lr = pw_ref[3, :, 0:n]
        li = pw_ref[3, :, n:2 * n]
        re, im = re + sr * lr - si * li, im + sr * li + si * lr
        x_ref[rows, 0:n] = re
        x_ref[rows, n:2 * n] = im
        return (jnp.broadcast_to(re[7:8, :], (8, n)), jnp.broadcast_to(im[7:8, :], (8, n)))

    sr, si = lax.fori_loop(0, ts // 8, group, (st_ref[:, 0:n], st_ref[:, n:2 * n]))
    st_ref[:, 0:n] = sr
    st_ref[:, n:2 * n] = si

    y = jnp.dot(x_ref[...].astype(BF16), cblk_ref[...], preferred_element_type=F32)
    y = _gelu(y + d_ref[...] * u)
    gate = jnp.dot(y.astype(BF16), gw_ref[...], preferred_element_type=F32) + gb_ref[...]
    o_ref[...] = y * _sigmoid(gate)


def _ssm_tables(lam_re, lam_im, log_dt, b_re, b_im, c_re, c_im):
    g, p = lam_re.shape
    hch = b_re.shape[2]
    n = g * p
    lam = lax.complex(jnp.minimum(lam_re.astype(F32), -1e-4), lam_im.astype(F32))
    dt = jnp.exp(log_dt.astype(F32))[:, None]
    lam_dt = lam * dt
    lam_bar = jnp.exp(lam_dt)
    b_bar = ((lam_bar - 1.0) / lam)[..., None] * lax.complex(b_re.astype(F32), b_im.astype(F32))
    eye = jnp.eye(g, dtype=F32)
    bb_re = jnp.einsum('gph,gk->ghkp', jnp.real(b_bar), eye).reshape(g * hch, n)
    bb_im = jnp.einsum('gph,gk->ghkp', jnp.imag(b_bar), eye).reshape(g * hch, n)
    bblk = jnp.concatenate([bb_re, bb_im], axis=1).astype(BF16)
    cc_re = jnp.einsum('ghp,gk->kpgh', c_re.astype(F32), eye).reshape(n, g * hch)
    cc_im = jnp.einsum('ghp,gk->kpgh', c_im.astype(F32), eye).reshape(n, g * hch)
    cblk = jnp.concatenate([cc_re, -cc_im], axis=0).astype(BF16)
    t = jnp.arange(8)
    tabs = []
    for d in (1, 2, 4):
        pw = jnp.exp(lam_dt * d).reshape(1, n)
        pw = jnp.where((t >= d)[:, None], pw, 0.0)
        tabs.append(jnp.concatenate([jnp.real(pw), jnp.imag(pw)], axis=1))
    pw = jnp.exp(lam_dt.reshape(1, n) * (t + 1).astype(F32)[:, None])
    tabs.append(jnp.concatenate([jnp.real(pw), jnp.imag(pw)], axis=1))
    return bblk, cblk, jnp.stack(tabs).astype(F32)


def s5_ssm(proj, batch, seq, col0, bblk, cblk, pw, d_skip, glu_w, glu_b, ts):
    ds = bblk.shape[0]
    n = bblk.shape[1] // 2
    nt = seq // ts
    kern = functools.partial(_ssm_kernel, ts=ts, n=n)
    cb = col0 // ds
    return pl.pallas_call(
        kern,
        grid=(batch, nt),
        in_specs=[pl.BlockSpec((ts, ds), lambda b, c: (b * nt + c, cb)),
                  pl.BlockSpec((ds, 2 * n), lambda b, c: (0, 0)),
                  pl.BlockSpec((4, 8, 2 * n), lambda b, c: (0, 0, 0)),
                  pl.BlockSpec((2 * n, ds), lambda b, c: (0, 0)),
                  pl.BlockSpec((1, ds), lambda b, c: (0, 0)),
                  pl.BlockSpec((ds, ds), lambda b, c: (0, 0)),
                  pl.BlockSpec((1, ds), lambda b, c: (0, 0))],
        out_specs=pl.BlockSpec((ts, ds), lambda b, c: (b * nt + c, 0)),
        out_shape=jax.ShapeDtypeStruct((batch * seq, ds), F32),
        scratch_shapes=[pltpu.VMEM((8, 2 * n), F32), pltpu.VMEM((ts, 2 * n), F32)],
        compiler_params=_params(("parallel", "arbitrary")),
    )(proj, bblk, pw, cblk, d_skip.reshape(1, ds), glu_w.astype(BF16), glu_b.reshape(1, ds))


def _mix_out_kernel(x_ref, yc_ref, ya_ref, ys_ref, g_ref, w_ref, o_ref, *, dc, da):
    acc = x_ref[...]
    off = 0
    for y_ref in (yc_ref, ya_ref, ys_ref):
        width = y_ref.shape[1]
        h = _rms(y_ref[...], g_ref[:, off:off + width]).astype(BF16)
        acc = acc + jnp.dot(h, w_ref[off:off + width, :], preferred_element_type=F32)
        off += width
    o_ref[...] = acc


def mix_out(x, yc, ya, ys, g, w_out, tm):
    m, d = x.shape
    dc, da, dsm = yc.shape[1], ya.shape[1], ys.shape[1]
    kern = functools.partial(_mix_out_kernel, dc=dc, da=da)
    return pl.pallas_call(
        kern,
        grid=(m // tm,),
        in_specs=[pl.BlockSpec((tm, d), lambda i: (i, 0)),
                  pl.BlockSpec((tm, dc), lambda i: (i, 0)),
                  pl.BlockSpec((tm, da), lambda i: (i, 0)),
                  pl.BlockSpec((tm, dsm), lambda i: (i, 0)),
                  pl.BlockSpec((1, dc + da + dsm), lambda i: (0, 0)),
                  pl.BlockSpec((dc + da + dsm, d), lambda i: (0, 0))],
        out_specs=pl.BlockSpec((tm, d), lambda i: (i, 0)),
        out_shape=jax.ShapeDtypeStruct((m, d), F32),
        compiler_params=_params(("parallel",)),
    )(x, yc, ya, ys, g.reshape(1, -1), w_out)


def _mem_attn_kernel(x_ref, g_ref, wq_ref, kv_ref, wo_ref, o_ref, *, heads):
    x = x_ref[...]
    d = x.shape[1]
    dh = d // heads
    h = _rms(x, g_ref[...]).astype(BF16)
    q = jnp.dot(h, wq_ref[...], preferred_element_type=F32) * (1.0 / math.sqrt(dh))
    outs = []
    for hd in range(heads):
        qh = q[:, hd * dh:(hd + 1) * dh].astype(BF16)
        kh = kv_ref[:, hd * dh:(hd + 1) * dh].astype(BF16)
        vh = kv_ref[:, d + hd * dh:d + (hd + 1) * dh].astype(BF16)
        s = lax.dot_general(qh, kh, (((1,), (1,)), ((), ())), preferred_element_type=F32)
        e = jnp.exp(s - jnp.max(s, axis=-1, keepdims=True))
        p = e / jnp.sum(e, axis=-1, keepdims=True)
        outs.append(jnp.dot(p.astype(BF16), vh, preferred_element_type=F32).astype(BF16))
    o = jnp.concatenate(outs, axis=-1)
    o_ref[...] = x + jnp.dot(o, wo_ref[...], preferred_element_type=F32)


def mem_attn(x, g, wq, memkv, wo, seq, n_mem, tm):
    m, d = x.shape
    tiles_per_seq = seq // tm
    kern = functools.partial(_mem_attn_kernel, heads=MEM_HEADS)
    return pl.pallas_call(
        kern,
        grid=(m // tm,),
        in_specs=[pl.BlockSpec((tm, d), lambda i: (i, 0)),
                  pl.BlockSpec((1, d), lambda i: (0, 0)),
                  pl.BlockSpec((d, d), lambda i: (0, 0)),
                  pl.BlockSpec((n_mem, 2 * d), lambda i: (i // tiles_per_seq, 0)),
                  pl.BlockSpec((d, d), lambda i: (0, 0))],
        out_specs=pl.BlockSpec((tm, d), lambda i: (i, 0)),
        out_shape=jax.ShapeDtypeStruct((m, d), F32),
        compiler_params=_params(("parallel",)),
    )(x, g.reshape(1, d), wq, memkv, wo)


def _top16(sc):
    rank = jnp.full(sc.shape, RANK_NONE, F32)
    vals = []
    for r in range(PEER_TOPK):
        m = jnp.max(sc, axis=0, keepdims=True)
        sel = sc == m
        rank = jnp.where(sel, float(r), rank)
        sc = jnp.where(sel, -jnp.inf, sc)
        vals.append(m)
    return vals, rank


def _peer_route_kernel(x_ref, g_ref, wq_ref, keys_ref, hb_ref, rank2_ref, e2_ref, cnt_ref, c1_ref,
                       q_scr, top_scr, rank_scr, sc_scr, *, tp):
    nk = PEER_KEYS
    h = _rms(x_ref[...], g_ref[...]).astype(BF16)
    hb_ref[...] = h
    q = jnp.dot(h, wq_ref[...], preferred_element_type=F32)
    for c in range(2 * PEER_HEADS):
        q_scr[c] = q[:, c * nk:(c + 1) * nk].astype(BF16)

    def side(c, _):
        sc = lax.dot_general(keys_ref[c], q_scr[c], (((1,), (1,)), ((), ())), preferred_element_type=F32)
        sc_scr[c] = sc
        vals, rank = _top16(sc)
        rank_scr[c] = rank
        for r in range(PEER_TOPK):
            top_scr[c, r:r + 1, :] = vals[r]
        return 0

    lax.fori_loop(0, 2 * PEER_HEADS, side, 0)

    def head(hd, _):
        a = top_scr[2 * hd]
        b = top_scr[2 * hd + 1]
        cands = [a[0:1, :] + b]
        for j in range(1, 8):
            cands.append(a[j:j + 1, :] + b[0:8, :])
        cands.append(a[8:16, :] + b[0:1, :])
        best, _ = _top16(jnp.concatenate(cands, axis=0))
        thr = best[PEER_TOPK - 1]
        zsum = jnp.zeros_like(thr)
        for r in range(PEER_TOPK):
            zsum = zsum + jnp.exp(best[r] - best[0])
        rank1 = rank_scr[2 * hd]
        cnt = jnp.zeros((nk, tp), F32)
        for j in range(PEER_TOPK):
            cj = jnp.sum(((a[j:j + 1, :] + b) >= thr).astype(F32), axis=0, keepdims=True)
            cnt = jnp.where(rank1 == float(j), cj, cnt)
        cnt_ref[hd] = cnt
        c1_ref[hd] = jnp.exp(sc_scr[2 * hd] - a[0:1, :]) / zsum
        rank2_ref[hd] = rank_scr[2 * hd + 1].astype(BF16)
        e2_ref[hd] = jnp.exp(sc_scr[2 * hd + 1] - b[0:1, :]).astype(BF16)
        return 0

    lax.fori_loop(0, PEER_HEADS, head, 0)


def peer_route(x, g, wq, keys, tp):
    m, d = x.shape
    nq = wq.shape[1]
    nk = PEER_KEYS
    kern = functools.partial(_peer_route_kernel, tp=tp)
    tok_blk = lambda i: (0, 0, i)
    return pl.pallas_call(
        kern,
        grid=(m // tp,),
        in_specs=[pl.BlockSpec((tp, d), lambda i: (i, 0)),
                  pl.BlockSpec((1, d), lambda i: (0, 0)),
                  pl.BlockSpec((d, nq), lambda i: (0, 0)),
                  pl.BlockSpec((2 * PEER_HEADS, nk, nk), lambda i: (0, 0, 0))],
        out_specs=[pl.BlockSpec((tp, d), lambda i: (i, 0)),
                   pl.BlockSpec((PEER_HEADS, nk, tp), tok_blk),
                   pl.BlockSpec((PEER_HEADS, nk, tp), tok_blk),
                   pl.BlockSpec((PEER_HEADS, nk, tp), tok_blk),
                   pl.BlockSpec((PEER_HEADS, nk, tp), tok_blk)],
        out_shape=[jax.ShapeDtypeStruct((m, d), BF16),
                   jax.ShapeDtypeStruct((PEER_HEADS, nk, m), BF16),
                   jax.ShapeDtypeStruct((PEER_HEADS, nk, m), BF16),
                   jax.ShapeDtypeStruct((PEER_HEADS, nk, m), F32),
                   jax.ShapeDtypeStruct((PEER_HEADS, nk, m), F32)],
        scratch_shapes=[pltpu.VMEM((2 * PEER_HEADS, tp, nk), BF16),
                        pltpu.VMEM((2 * PEER_HEADS, PEER_TOPK, tp), F32),
                        pltpu.VMEM((2 * PEER_HEADS, nk, tp), F32),
                        pltpu.VMEM((2 * PEER_HEADS, nk, tp), F32)],
        compiler_params=_params(("parallel",)),
    )(x, g.reshape(1, d), wq, keys)


def _peer_experts_kernel(x_ref, hb_ref, rank2_ref, e2_ref, cnt_ref, c1_ref, u_ref, vt_ref, o_ref,
                         acc_ref, hw_ref, *, blocks):
    e = pl.program_id(1)
    nk = PEER_KEYS

    @pl.when(e == 0)
    def _():
        acc_ref[...] = jnp.zeros_like(acc_ref)

    pre = lax.dot_general(u_ref[...], hb_ref[...], (((1,), (1,)), ((), ())), preferred_element_type=F32)
    for j in range(blocks):
        w = None
        for hd in range(PEER_HEADS):
            sel = rank2_ref[hd].astype(F32) < cnt_ref[hd, j:j + 1, :]
            term = jnp.where(sel, e2_ref[hd].astype(F32), 0.0) * c1_ref[hd, j:j + 1, :]
            w = term if w is None else w + term
        hw_ref[j * nk:(j + 1) * nk, :] = (_gelu(pre[j * nk:(j + 1) * nk, :]) * w).astype(BF16)
    acc_ref[...] += jnp.dot(vt_ref[...], hw_ref[...], preferred_element_type=F32)

    @pl.when(e == pl.num_programs(1) - 1)
    def _():
        o_ref[...] = x_ref[...] + acc_ref[...].T


def peer_experts(x, hb, rank2, e2, cnt, c1, u_tab, vt_tab, tt, blocks):
    m, d = x.shape
    n_exp = u_tab.shape[0]
    nk = PEER_KEYS
    te = blocks * nk
    kern = functools.partial(_peer_experts_kernel, blocks=blocks)
    return pl.pallas_call(
        kern,
        grid=(m // tt, n_exp // te),
        in_specs=[pl.BlockSpec((tt, d), lambda t, e: (t, 0)),
                  pl.BlockSpec((tt, d), lambda t, e: (t, 0)),
                  pl.BlockSpec((PEER_HEADS, nk, tt), lambda t, e: (0, 0, t)),
                  pl.BlockSpec((PEER_HEADS, nk, tt), lambda t, e: (0, 0, t)),
                  pl.BlockSpec((PEER_HEADS, blocks, tt), lambda t, e: (0, e, t)),
                  pl.BlockSpec((PEER_HEADS, blocks, tt), lambda t, e: (0, e, t)),
                  pl.BlockSpec((te, d), lambda t, e: (e, 0)),
                  pl.BlockSpec((d, te), lambda t, e: (0, e))],
        out_specs=pl.BlockSpec((tt, d), lambda t, e: (t, 0)),
        out_shape=jax.ShapeDtypeStruct((m, d), F32),
        scratch_shapes=[pltpu.VMEM((d, tt), F32), pltpu.VMEM((te, tt), BF16)],
        compiler_params=_params(("parallel", "arbitrary")),
    )(x, hb, rank2, e2, cnt, c1, u_tab, vt_tab)


def _final_norm_kernel(x_ref, g_ref, o_ref):
    o_ref[...] = _rms(x_ref[...], g_ref[...])


def final_norm(x, g, tm):
    m, d = x.shape
    return pl.pallas_call(
        _final_norm_kernel,
        grid=(m // tm,),
        in_specs=[pl.BlockSpec((tm, d), lambda i: (i, 0)), pl.BlockSpec((1, d), lambda i: (0, 0))],
        out_specs=pl.BlockSpec((tm, d), lambda i: (i, 0)),
        out_shape=jax.ShapeDtypeStruct((m, d), F32),
        compiler_params=_params(("parallel",)),
    )(x, g.reshape(1, d))


def _tiles(seq, n_mem):
    return dict(
        proj=min(512, seq), conv=min(512, seq), sba=min(256, seq), ssm=min(512, seq), mix=min(512, seq),
        mem=min(512, seq), memkv=min(256, n_mem), route=min(256, seq), experts=min(1024, seq), blocks=8,
        final=min(512, seq))


def kernel(x, mem, norm_mix_g, w_in, conv_w, conv_b, conv_ln_g, conv_ln_b, ssm_lambda_re, ssm_lambda_im, ssm_log_dt, ssm_b_re, ssm_b_im, ssm_c_re, ssm_c_im, ssm_d, ssm_glu_w, ssm_glu_b, grp_norm_g, w_out, norm_mem_g, mem_norm_g, w_mq, w_mkv, w_mo, norm_ffn_g, peer_wq, peer_sub_keys, peer_u, peer_v, final_norm_g):
    batch, seq, d = x.shape
    n_mem = mem.shape[1]
    depth = w_in.shape[0]
    d_conv = conv_w.shape[2]
    d_ssm = ssm_d.shape[1]
    d_attn = (w_in.shape[2] - 2 * d_conv - d_ssm) // 3
    t = _tiles(seq, n_mem)

    xf = x.reshape(batch * seq, d)
    memf = mem.reshape(batch * n_mem, d)
    for l in range(depth):
        proj = norm_matmul(xf, norm_mix_g[l], w_in[l].astype(BF16), t['proj'])
        yc = conformer_conv(proj, conv_w[l], conv_b[l], conv_ln_g[l], conv_ln_b[l], seq, t['conv'])
        ya = stick_breaking_attention(proj, batch, seq, 2 * d_conv, d_attn, t['sba'])
        bblk, cblk, pw = _ssm_tables(ssm_lambda_re[l], ssm_lambda_im[l], ssm_log_dt[l], ssm_b_re[l], ssm_b_im[l],
                                     ssm_c_re[l], ssm_c_im[l])
        ys = s5_ssm(proj, batch, seq, 2 * d_conv + 3 * d_attn, bblk, cblk, pw, ssm_d[l], ssm_glu_w[l], ssm_glu_b[l],
                    t['ssm'])
        xf = mix_out(xf, yc, ya, ys, grp_norm_g[l], w_out[l].astype(BF16), t['mix'])

        memkv = norm_matmul(memf, mem_norm_g[l], w_mkv[l].astype(BF16), t['memkv'])
        xf = mem_attn(xf, norm_mem_g[l], w_mq[l].astype(BF16), memkv, w_mo[l].astype(BF16), seq, n_mem, t['mem'])

        keys = peer_sub_keys[l].reshape(2 * PEER_HEADS, PEER_KEYS, -1).astype(BF16)
        hb, rank2, e2, cnt, c1 = peer_route(xf, norm_ffn_g[l], peer_wq[l].astype(BF16), keys, t['route'])
        xf = peer_experts(xf, hb, rank2, e2, cnt, c1, peer_u[l].astype(BF16), peer_v[l].T.astype(BF16),
                          t['experts'], t['blocks'])
    return final_norm(xf, final_norm_g, t['final']).reshape(batch, seq, d)
```

```python
import functools
import math

import jax
import jax.numpy as jnp
from jax import lax
from jax.experimental import pallas as pl
from jax.experimental.pallas import tpu as pltpu

F32 = jnp.float32
BF16 = jnp.bfloat16
EPS = 1e-6

CONV_WIDTH = 31
CONV_HALO = 32
ATTN_HEAD_DIM = 64
SBA_HEADS_PER_STEP = 4
LOG2E = 1.4426950408889634
UNDERFLOW_LOG2 = -150.0
SSM_GROUP = 16
SSM_STATE = 64
MEM_HEADS = 4
PEER_HEADS = 8
PEER_KEYS = 128
PEER_TOPK = 16
RANK_NONE = 99.0
BF16_ROWS = 16
VMEM_LIMIT = 50 * 1024 * 1024


def _params(sem, vmem=VMEM_LIMIT):
    return pltpu.CompilerParams(dimension_semantics=sem, vmem_limit_bytes=vmem)


def _rms(x, g):
    ms = jnp.mean(x * x, axis=-1, keepdims=True)
    return x * lax.rsqrt(ms + EPS) * g


def _gelu(x):
    c = math.sqrt(2.0 / math.pi)
    return 0.5 * x * (1.0 + jnp.tanh(c * (x + 0.044715 * (x * x * x))))


def _sigmoid(x):
    return 1.0 / (1.0 + jnp.exp(-x))


def _norm_matmul_kernel(x_ref, g_ref, w_ref, o_ref):
    h = _rms(x_ref[...], g_ref[...]).astype(BF16)
    o_ref[...] = jnp.dot(h, w_ref[...], preferred_element_type=F32)


def norm_matmul(x, g, w, tm):
    m, d = x.shape
    n = w.shape[1]
    return pl.pallas_call(
        _norm_matmul_kernel,
        grid=(m // tm,),
        in_specs=[pl.BlockSpec((tm, d), lambda i: (i, 0)),
                  pl.BlockSpec((1, d), lambda i: (0, 0)),
                  pl.BlockSpec((d, n), lambda i: (0, 0))],
        out_specs=pl.BlockSpec((tm, n), lambda i: (i, 0)),
        out_shape=jax.ShapeDtypeStruct((m, n), F32),
        compiler_params=_params(("parallel",)),
    )(x, g.reshape(1, d), w)


def _conv_kernel(cur_ref, halo_ref, w_ref, b_ref, lng_ref, lnb_ref, o_ref, ext_ref, *, ts, tiles_per_seq, dc):
    i = pl.program_id(0)
    cur = cur_ref[...]
    halo = halo_ref[...]
    y = cur[:, :dc] * _sigmoid(cur[:, dc:])
    yh = halo[:, :dc] * _sigmoid(halo[:, dc:])
    yh = jnp.where(i % tiles_per_seq == 0, 0.0, yh)
    ext_ref[0:CONV_HALO, :] = yh
    ext_ref[CONV_HALO:, :] = y
    acc = jnp.broadcast_to(b_ref[...], (ts, dc))
    for k in range(CONV_WIDTH):
        off = CONV_HALO - (CONV_WIDTH - 1) + k
        acc = acc + w_ref[k:k + 1, :] * ext_ref[off:off + ts, :]
    mu = jnp.mean(acc, axis=-1, keepdims=True)
    cen = acc - mu
    var = jnp.mean(cen * cen, axis=-1, keepdims=True)
    z = cen * lax.rsqrt(var + EPS) * lng_ref[...] + lnb_ref[...]
    o_ref[...] = z * _sigmoid(z)


def conformer_conv(proj, conv_w, conv_b, ln_g, ln_b, seq, ts):
    m = proj.shape[0]
    dc = conv_w.shape[1]
    kern = functools.partial(_conv_kernel, ts=ts, tiles_per_seq=seq // ts, dc=dc)
    halo_blocks = ts // CONV_HALO
    return pl.pallas_call(
        kern,
        grid=(m // ts,),
        in_specs=[pl.BlockSpec((ts, 2 * dc), lambda i: (i, 0)),
                  pl.BlockSpec((CONV_HALO, 2 * dc), lambda i: (jnp.maximum(i * halo_blocks - 1, 0), 0)),
                  pl.BlockSpec((CONV_WIDTH, dc), lambda i: (0, 0)),
                  pl.BlockSpec((1, dc), lambda i: (0, 0)),
                  pl.BlockSpec((1, dc), lambda i: (0, 0)),
                  pl.BlockSpec((1, dc), lambda i: (0, 0))],
        out_specs=pl.BlockSpec((ts, dc), lambda i: (i, 0)),
        out_shape=jax.ShapeDtypeStruct((m, dc), F32),
        scratch_shapes=[pltpu.VMEM((ts + CONV_HALO, dc), F32)],
        compiler_params=_params(("parallel",)),
    )(proj, proj, conv_w, conv_b.reshape(1, dc), ln_g.reshape(1, dc), ln_b.reshape(1, dc))


def _sba_kernel(q_ref, k_ref, v_ref, o_ref, *, tq, scale):
    i = pl.program_id(2)
    dh = ATTN_HEAD_DIM
    row = lax.broadcasted_iota(jnp.int32, (tq, tq), 0)
    col = lax.broadcasted_iota(jnp.int32, (tq, tq), 1)
    causal = col < row
    suffix = (row > col).astype(BF16)

    heads = q_ref.shape[1] // dh
    lanes = [slice(hh * dh, (hh + 1) * dh) for hh in range(heads)]
    qs = [(q_ref[:, ln] * (scale * LOG2E)).astype(BF16) for ln in lanes]

    def block(j, carry, diagonal):
        start = pl.multiple_of(j * tq, tq)
        hs = range(heads)
        kj = [k_ref[pl.ds(start, tq), lanes[h]].astype(BF16) for h in hs]
        vj = [v_ref[pl.ds(start, tq), lanes[h]].astype(BF16) for h in hs]
        z = [lax.dot_general(qs[h], kj[h], (((1,), (1,)), ((), ())), preferred_element_type=F32) for h in hs]
        log_beta, log_stay, later = [], [], []
        for h in hs:
            sp = jnp.log2(1.0 + jnp.exp2(-jnp.abs(z[h])))
            log_beta.append(jnp.minimum(z[h], 0.0) - sp)
            stay = -jnp.maximum(z[h], 0.0) - sp
            log_stay.append(jnp.where(causal, stay, 0.0) if diagonal else stay)
        for h in hs:
            hi = log_stay[h].astype(BF16)
            lo = (log_stay[h] - hi.astype(F32)).astype(BF16)
            later.append(jnp.dot(hi, suffix, preferred_element_type=F32)
                         + jnp.dot(lo, suffix, preferred_element_type=F32))
        out = []
        for h in hs:
            acc, run = carry[h]
            a = jnp.exp2(log_beta[h] + later[h] + run)
            if diagonal:
                a = jnp.where(causal, a, 0.0)
            acc = acc + jnp.dot(a.astype(BF16), vj[h], preferred_element_type=F32)
            run = run + later[h][:, 0:1] + log_stay[h][:, 0:1]
            out.append((acc, run))
        return tuple(out)

    def unfinished(c):
        return jnp.logical_and(c[0] < i, c[1])

    def sweep(c):
        jj, _, carry = c
        carry = block(i - 1 - jj, carry, False)
        top = carry[0][1]
        for h in range(1, heads):
            top = jnp.maximum(top, carry[h][1])
        return jj + 1, jnp.max(top) > UNDERFLOW_LOG2, carry

    carry = tuple((jnp.zeros((tq, dh), F32), jnp.zeros((tq, 1), F32)) for _ in range(heads))
    carry = block(i, carry, True)
    _, _, carry = lax.while_loop(unfinished, sweep, (jnp.int32(0), jnp.bool_(True), carry))
    for hh in range(heads):
        o_ref[:, lanes[hh]] = carry[hh][0]


def stick_breaking_attention(proj, batch, seq, col0, d_attn, tq):
    lane_blk = SBA_HEADS_PER_STEP * ATTN_HEAD_DIM
    hp = d_attn // lane_blk
    qb, kb, vb = col0 // lane_blk, (col0 + d_attn) // lane_blk, (col0 + 2 * d_attn) // lane_blk
    nq = seq // tq
    kern = functools.partial(_sba_kernel, tq=tq, scale=1.0 / math.sqrt(ATTN_HEAD_DIM))
    return pl.pallas_call(
        kern,
        grid=(batch, hp, nq),
        in_specs=[pl.BlockSpec((tq, lane_blk), lambda b, h, i: (b * nq + i, qb + h)),
                  pl.BlockSpec((seq, lane_blk), lambda b, h, i: (b, kb + h)),
                  pl.BlockSpec((seq, lane_blk), lambda b, h, i: (b, vb + h))],
        out_specs=pl.BlockSpec((tq, lane_blk), lambda b, h, i: (b * nq + i, h)),
        out_shape=jax.ShapeDtypeStruct((batch * seq, d_attn), F32),
        compiler_params=_params(("parallel", "parallel", "arbitrary")),
    )(proj, proj, proj)


def _ssm_kernel(u_ref, bblk_ref, pw_ref, cblk_ref, d_ref, gw_ref, gb_ref, o_ref, st_ref, x_ref, *, ts, n):
    c = pl.program_id(1)

    @pl.when(c == 0)
    def _():
        st_ref[...] = jnp.zeros_like(st_ref)

    u = u_ref[...]
    x_ref[...] = jnp.dot(u.astype(BF16), bblk_ref[...], preferred_element_type=F32)

    def group(r, carry):
        sr, si = carry
        rows = pl.ds(pl.multiple_of(r * 8, 8), 8)
        re = x_ref[rows, 0:n]
        im = x_ref[rows, n:2 * n]
        for lvl, d in enumerate((1, 2, 4)):
            lr = pw_ref[lvl, :, 0:n]
            li = pw_ref[lvl, :, n:2 * n]
            pr = pltpu.roll(re, d, 0)
            pi = pltpu.roll(im, d, 0)
            re, im = re + pr * lr - pi * li, im + pr * li + pi * lr
        lr = pw_ref[3, :, 0:n]
        li = pw_ref[3, :, n:2 * n]
        re, im = re + sr * lr - si * li, im + sr * li + si * lr
        x_ref[rows, 0:n] = re
        x_ref[rows, n:2 * n] = im
        return (jnp.broadcast_to(re[7:8, :], (8, n)), jnp.broadcast_to(im[7:8, :], (8, n)))

    sr, si = lax.fori_loop(0, ts // 8, group, (st_ref[:, 0:n], st_ref[:, n:2 * n]))
    st_ref[:, 0:n] = sr
    st_ref[:, n:2 * n] = si

    y = jnp.dot(x_ref[...].astype(BF16), cblk_ref[...], preferred_element_type=F32)
    y = _gelu(y + d_ref[...] * u)
    gate = jnp.dot(y.astype(BF16), gw_ref[...], preferred_element_type=F32) + gb_ref[...]
    o_ref[...] = y * _sigmoid(gate)


def _ssm_tables(lam_re, lam_im, log_dt, b_re, b_im, c_re, c_im):
    g, p = lam_re.shape
    hch = b_re.shape[2]
    n = g * p
    lr = jnp.minimum(lam_re.astype(F32), -1e-4)
    li = lam_im.astype(F32)
    dt = jnp.exp(log_dt.astype(F32))[:, None]

    def lam_pow(steps):
        mag = jnp.exp(lr * dt * steps)
        return mag * jnp.cos(li * dt * steps), mag * jnp.sin(li * dt * steps)

    br, bi = lam_pow(1.0)
    norm = lr * lr + li * li
    cr = ((br - 1.0) * lr + bi * li) / norm
    ci = (bi * lr - (br - 1.0) * li) / norm
    bre, bim = b_re.astype(F32), b_im.astype(F32)
    bbar_re = cr[..., None] * bre - ci[..., None] * bim
    bbar_im = cr[..., None] * bim + ci[..., None] * bre
    eye = jnp.eye(g, dtype=F32)
    bb_re = jnp.einsum('gph,gk->ghkp', bbar_re, eye).reshape(g * hch, n)
    bb_im = jnp.einsum('gph,gk->ghkp', bbar_im, eye).reshape(g * hch, n)
    bblk = jnp.concatenate([bb_re, bb_im], axis=1).astype(BF16)
    cc_re = jnp.einsum('ghp,gk->kpgh', c_re.astype(F32), eye).reshape(n, g * hch)
    cc_im = jnp.einsum('ghp,gk->kpgh', c_im.astype(F32), eye).reshape(n, g * hch)
    cblk = jnp.concatenate([cc_re, -cc_im], axis=0).astype(BF16)
    t = jnp.arange(8)
    tabs = []
    for d in (1, 2, 4):
        pr, pi = lam_pow(float(d))
        keep = (t >= d)[:, None]
        tabs.append(jnp.concatenate([jnp.where(keep, pr.reshape(1, n), 0.0),
                                     jnp.where(keep, pi.reshape(1, n), 0.0)], axis=1))
    rows = [lam_pow(float(k + 1)) for k in range(8)]
    tabs.append(jnp.concatenate([jnp.stack([r[0].reshape(n) for r in rows]),
                                 jnp.stack([r[1].reshape(n) for r in rows])], axis=1))
    return bblk, cblk, jnp.stack(tabs).astype(F32)


def s5_ssm(proj, batch, seq, col0, bblk, cblk, pw, d_skip, glu_w, glu_b, ts):
    ds = bblk.shape[0]
    n = bblk.shape[1] // 2
    nt = seq // ts
    kern = functools.partial(_ssm_kernel, ts=ts, n=n)
    cb = col0 // ds
    return pl.pallas_call(
        kern,
        grid=(batch, nt),
        in_specs=[pl.BlockSpec((ts, ds), lambda b, c: (b * nt + c, cb)),
                  pl.BlockSpec((ds, 2 * n), lambda b, c: (0, 0)),
                  pl.BlockSpec((4, 8, 2 * n), lambda b, c: (0, 0, 0)),
                  pl.BlockSpec((2 * n, ds), lambda b, c: (0, 0)),
                  pl.BlockSpec((1, ds), lambda b, c: (0, 0)),
                  pl.BlockSpec((ds, ds), lambda b, c: (0, 0)),
                  pl.BlockSpec((1, ds), lambda b, c: (0, 0))],
        out_specs=pl.BlockSpec((ts, ds), lambda b, c: (b * nt + c, 0)),
        out_shape=jax.ShapeDtypeStruct((batch * seq, ds), F32),
        scratch_shapes=[pltpu.VMEM((8, 2 * n), F32), pltpu.VMEM((ts, 2 * n), F32)],
        compiler_params=_params(("parallel", "arbitrary")),
    )(proj, bblk, pw, cblk, d_skip.reshape(1, ds), glu_w.astype(BF16), glu_b.reshape(1, ds))


def _mix_out_kernel(x_ref, yc_ref, ya_ref, ys_ref, g_ref, w_ref, o_ref, *, dc, da):
    acc = x_ref[...]
    off = 0
    for y_ref in (yc_ref, ya_ref, ys_ref):
        width = y_ref.shape[1]
        h = _rms(y_ref[...], g_ref[:, off:off + width]).astype(BF16)
        acc = acc + jnp.dot(h, w_ref[off:off + width, :], preferred_element_type=F32)
        off += width
    o_ref[...] = acc


def mix_out(x, yc, ya, ys, g, w_out, tm):
    m, d = x.shape
    dc, da, dsm = yc.shape[1], ya.shape[1], ys.shape[1]
    kern = functools.partial(_mix_out_kernel, dc=dc, da=da)
    return pl.pallas_call(
        kern,
        grid=(m // tm,),
        in_specs=[pl.BlockSpec((tm, d), lambda i: (i, 0)),
                  pl.BlockSpec((tm, dc), lambda i: (i, 0)),
                  pl.BlockSpec((tm, da), lambda i: (i, 0)),
                  pl.BlockSpec((tm, dsm), lambda i: (i, 0)),
                  pl.BlockSpec((1, dc + da + dsm), lambda i: (0, 0)),
                  pl.BlockSpec((dc + da + dsm, d), lambda i: (0, 0))],
        out_specs=pl.BlockSpec((tm, d), lambda i: (i, 0)),
        out_shape=jax.ShapeDtypeStruct((m, d), F32),
        compiler_params=_params(("parallel",)),
    )(x, yc, ya, ys, g.reshape(1, -1), w_out)


def _mem_attn_kernel(x_ref, g_ref, wq_ref, kv_ref, wo_ref, o_ref, *, heads):
    x = x_ref[...]
    d = x.shape[1]
    dh = d // heads
    h = _rms(x, g_ref[...]).astype(BF16)
    q = jnp.dot(h, wq_ref[...], preferred_element_type=F32) * (1.0 / math.sqrt(dh))
    outs = []
    for hd in range(heads):
        qh = q[:, hd * dh:(hd + 1) * dh].astype(BF16)
        kh = kv_ref[:, hd * dh:(hd + 1) * dh].astype(BF16)
        vh = kv_ref[:, d + hd * dh:d + (hd + 1) * dh].astype(BF16)
        s = lax.dot_general(qh, kh, (((1,), (1,)), ((), ())), preferred_element_type=F32)
        e = jnp.exp(s - jnp.max(s, axis=-1, keepdims=True))
        p = e / jnp.sum(e, axis=-1, keepdims=True)
        outs.append(jnp.dot(p.astype(BF16), vh, preferred_element_type=F32).astype(BF16))
    o = jnp.concatenate(outs, axis=-1)
    o_ref[...] = x + jnp.dot(o, wo_ref[...], preferred_element_type=F32)


def mem_attn(x, g, wq, memkv, wo, seq, n_mem, tm):
    m, d = x.shape
    tiles_per_seq = seq // tm
    kern = functools.partial(_mem_attn_kernel, heads=MEM_HEADS)
    return pl.pallas_call(
        kern,
        grid=(m // tm,),
        in_specs=[pl.BlockSpec((tm, d), lambda i: (i, 0)),
                  pl.BlockSpec((1, d), lambda i: (0, 0)),
                  pl.BlockSpec((d, d), lambda i: (0, 0)),
                  pl.BlockSpec((n_mem, 2 * d), lambda i: (i // tiles_per_seq, 0)),
                  pl.BlockSpec((d, d), lambda i: (0, 0))],
        out_specs=pl.BlockSpec((tm, d), lambda i: (i, 0)),
        out_shape=jax.ShapeDtypeStruct((m, d), F32),
        compiler_params=_params(("parallel",)),
    )(x, g.reshape(1, d), wq, memkv, wo)


def _top16(scs, want_rank):
    ranks = [jnp.full(sc.shape, RANK_NONE, F32) if w else None for sc, w in zip(scs, want_rank)]
    vals = [[] for _ in scs]
    scs = list(scs)
    for r in range(PEER_TOPK):
        for n in range(len(scs)):
            m = jnp.max(scs[n], axis=0, keepdims=True)
            sel = scs[n] == m
            if want_rank[n]:
                ranks[n] = jnp.where(sel, float(r), ranks[n])
            scs[n] = jnp.where(sel, -jnp.inf, scs[n])
            vals[n].append(m)
    return vals, ranks


def _peer_route_kernel(x_ref, g_ref, wq_ref, keys_ref, hb_ref, rank2_ref, e2_ref, cnt_ref, c1_ref,
                       q_scr, top_scr, *, tp):
    nk = PEER_KEYS
    h = _rms(x_ref[...], g_ref[...]).astype(BF16)
    hb_ref[...] = h
    q = jnp.dot(h, wq_ref[...], preferred_element_type=F32)
    for c in range(2 * PEER_HEADS):
        q_scr[c] = q[:, c * nk:(c + 1) * nk].astype(BF16)

    def head(hd, _):
        nt = (((1,), (1,)), ((), ()))
        s1 = lax.dot_general(keys_ref[2 * hd], q_scr[2 * hd], nt, preferred_element_type=F32)
        s2 = lax.dot_general(keys_ref[2 * hd + 1], q_scr[2 * hd + 1], nt, preferred_element_type=F32)
        (va, vb), (_, rank2) = _top16([s1, s2], [False, True])
        for r in range(PEER_TOPK):
            top_scr[0, r:r + 1, :] = va[r]
            top_scr[1, r:r + 1, :] = vb[r]
        a = top_scr[0]
        b = top_scr[1]
        cands = [a[0:1, :] + b]
        for j in range(1, 8):
            cands.append(a[j:j + 1, :] + b[0:8, :])
        cands.append(a[8:16, :] + b[0:1, :])
        (best,), _ = _top16([jnp.concatenate(cands, axis=0)], [False])
        thr = best[PEER_TOPK - 1]
        zsum = jnp.zeros_like(thr)
        for r in range(PEER_TOPK):
            zsum = zsum + jnp.exp(best[r] - best[0])
        cnt = jnp.zeros((nk, tp), F32)
        for j in range(PEER_TOPK):
            cj = jnp.sum(((a[j:j + 1, :] + b) >= thr).astype(F32), axis=0, keepdims=True)
            cnt = jnp.where(s1 == a[j:j + 1, :], cj, cnt)
        cnt_ref[hd] = cnt
        c1_ref[hd] = jnp.exp(s1 - a[0:1, :]) / zsum
        packed = (nk // BF16_ROWS, BF16_ROWS, tp)
        rank2_ref[hd] = rank2.reshape(packed).astype(BF16)
        e2_ref[hd] = jnp.exp(s2 - b[0:1, :]).reshape(packed).astype(BF16)
        return 0

    lax.fori_loop(0, PEER_HEADS, head, 0)


def peer_route(x, g, wq, keys, tp):
    m, d = x.shape
    nq = wq.shape[1]
    nk = PEER_KEYS
    kern = functools.partial(_peer_route_kernel, tp=tp)
    tok_blk = lambda i: (0, 0, i)
    packed_blk = lambda i: (0, 0, 0, i)
    packed = (PEER_HEADS, nk // BF16_ROWS, BF16_ROWS)
    return pl.pallas_call(
        kern,
        grid=(m // tp,),
        in_specs=[pl.BlockSpec((tp, d), lambda i: (i, 0)),
                  pl.BlockSpec((1, d), lambda i: (0, 0)),
                  pl.BlockSpec((d, nq), lambda i: (0, 0)),
                  pl.BlockSpec((2 * PEER_HEADS, nk, nk), lambda i: (0, 0, 0))],
        out_specs=[pl.BlockSpec((tp, d), lambda i: (i, 0)),
                   pl.BlockSpec(packed + (tp,), packed_blk),
                   pl.BlockSpec(packed + (tp,), packed_blk),
                   pl.BlockSpec((PEER_HEADS, nk, tp), tok_blk),
                   pl.BlockSpec((PEER_HEADS, nk, tp), tok_blk)],
        out_shape=[jax.ShapeDtypeStruct((m, d), BF16),
                   jax.ShapeDtypeStruct(packed + (m,), BF16),
                   jax.ShapeDtypeStruct(packed + (m,), BF16),
                   jax.ShapeDtypeStruct((PEER_HEADS, nk, m), F32),
                   jax.ShapeDtypeStruct((PEER_HEADS, nk, m), F32)],
        scratch_shapes=[pltpu.VMEM((2 * PEER_HEADS, tp, nk), BF16),
                        pltpu.VMEM((2, PEER_TOPK, tp), F32)],
        compiler_params=_params(("parallel",)),
    )(x, g.reshape(1, d), wq, keys)


def _peer_experts_kernel(x_ref, hb_ref, rank2_ref, e2_ref, cnt_ref, c1_ref, u_ref, vt_ref, o_ref,
                         acc_ref, hw_ref, *, blocks):
    e = pl.program_id(1)
    nk = PEER_KEYS
    tt = hb_ref.shape[0]
    tiles = nk // BF16_ROWS

    @pl.when(e == 0)
    def _():
        acc_ref[...] = jnp.zeros_like(acc_ref)

    pre = lax.dot_general(u_ref[...], hb_ref[...], (((1,), (1,)), ((), ())), preferred_element_type=F32)
    for j in range(blocks):
        w = None
        for hd in range(PEER_HEADS):
            cnt = jnp.broadcast_to(cnt_ref[hd, j:j + 1, :], (BF16_ROWS, tt)).astype(BF16)
            c1 = jnp.broadcast_to(c1_ref[hd, j:j + 1, :], (BF16_ROWS, tt)).astype(BF16)
            term = jnp.where(rank2_ref[hd] < cnt[None], e2_ref[hd], jnp.zeros((), BF16)) * c1[None]
            w = term if w is None else w + term
        act = _gelu(pre[j * nk:(j + 1) * nk, :].reshape(tiles, BF16_ROWS, tt).astype(BF16))
        hw_ref[j * tiles:(j + 1) * tiles] = act * w
    hw = hw_ref[...].reshape(blocks * nk, tt)
    acc_ref[...] += jnp.dot(vt_ref[...], hw, preferred_element_type=F32)

    @pl.when(e == pl.num_programs(1) - 1)
    def _():
        o_ref[...] = x_ref[...] + acc_ref[...].T


def peer_experts(x, hb, rank2, e2, cnt, c1, u_tab, vt_tab, tt, blocks):
    m, d = x.shape
    n_exp = u_tab.shape[0]
    nk = PEER_KEYS
    te = blocks * nk
    kern = functools.partial(_peer_experts_kernel, blocks=blocks)
    packed = (PEER_HEADS, nk // BF16_ROWS, BF16_ROWS, tt)
    return pl.pallas_call(
        kern,
        grid=(m // tt, n_exp // te),
        in_specs=[pl.BlockSpec((tt, d), lambda t, e: (t, 0)),
                  pl.BlockSpec((tt, d), lambda t, e: (t, 0)),
                  pl.BlockSpec(packed, lambda t, e: (0, 0, 0, t)),
                  pl.BlockSpec(packed, lambda t, e: (0, 0, 0, t)),
                  pl.BlockSpec((PEER_HEADS, blocks, tt), lambda t, e: (0, e, t)),
                  pl.BlockSpec((PEER_HEADS, blocks, tt), lambda t, e: (0, e, t)),
                  pl.BlockSpec((te, d), lambda t, e: (e, 0)),
                  pl.BlockSpec((d, te), lambda t, e: (0, e))],
        out_specs=pl.BlockSpec((tt, d), lambda t, e: (t, 0)),
        out_shape=jax.ShapeDtypeStruct((m, d), F32),
        scratch_shapes=[pltpu.VMEM((d, tt), F32), pltpu.VMEM((te // BF16_ROWS, BF16_ROWS, tt), BF16)],
        compiler_params=_params(("parallel", "arbitrary")),
    )(x, hb, rank2, e2, cnt, c1, u_tab, vt_tab)


def _final_norm_kernel(x_ref, g_ref, o_ref):
    o_ref[...] = _rms(x_ref[...], g_ref[...])


def final_norm(x, g, tm):
    m, d = x.shape
    return pl.pallas_call(
        _final_norm_kernel,
        grid=(m // tm,),
        in_specs=[pl.BlockSpec((tm, d), lambda i: (i, 0)), pl.BlockSpec((1, d), lambda i: (0, 0))],
        out_specs=pl.BlockSpec((tm, d), lambda i: (i, 0)),
        out_shape=jax.ShapeDtypeStruct((m, d), F32),
        compiler_params=_params(("parallel",)),
    )(x, g.reshape(1, d))


def _tiles(seq, n_mem):
    return dict(
        proj=min(512, seq), conv=min(512, seq), sba=min(256, seq), ssm=min(512, seq), mix=min(512, seq),
        mem=min(512, seq), memkv=min(256, n_mem), route=min(512, seq), experts=min(1024, seq), blocks=8,
        final=min(512, seq))


def kernel(x, mem, norm_mix_g, w_in, conv_w, conv_b, conv_ln_g, conv_ln_b, ssm_lambda_re, ssm_lambda_im, ssm_log_dt, ssm_b_re, ssm_b_im, ssm_c_re, ssm_c_im, ssm_d, ssm_glu_w, ssm_glu_b, grp_norm_g, w_out, norm_mem_g, mem_norm_g, w_mq, w_mkv, w_mo, norm_ffn_g, peer_wq, peer_sub_keys, peer_u, peer_v, final_norm_g):
    batch, seq, d = x.shape
    n_mem = mem.shape[1]
    depth = w_in.shape[0]
    d_conv = conv_w.shape[2]
    d_ssm = ssm_d.shape[1]
    d_attn = (w_in.shape[2] - 2 * d_conv - d_ssm) // 3
    t = _tiles(seq, n_mem)

    xf = x.reshape(batch * seq, d)
    memf = mem.reshape(batch * n_mem, d)
    for l in range(depth):
        proj = norm_matmul(xf, norm_mix_g[l], w_in[l].astype(BF16), t['proj'])
        yc = conformer_conv(proj, conv_w[l], conv_b[l], conv_ln_g[l], conv_ln_b[l], seq, t['conv'])
        ya = stick_breaking_attention(proj, batch, seq, 2 * d_conv, d_attn, t['sba'])
        bblk, cblk, pw = _ssm_tables(ssm_lambda_re[l], ssm_lambda_im[l], ssm_log_dt[l], ssm_b_re[l], ssm_b_im[l],
                                     ssm_c_re[l], ssm_c_im[l])
        ys = s5_ssm(proj, batch, seq, 2 * d_conv + 3 * d_attn, bblk, cblk, pw, ssm_d[l], ssm_glu_w[l], ssm_glu_b[l],
                    t['ssm'])
        xf = mix_out(xf, yc, ya, ys, grp_norm_g[l], w_out[l].astype(BF16), t['mix'])

        memkv = norm_matmul(memf, mem_norm_g[l], w_mkv[l].astype(BF16), t['memkv'])
        xf = mem_attn(xf, norm_mem_g[l], w_mq[l].astype(BF16), memkv, w_mo[l].astype(BF16), seq, n_mem, t['mem'])

        keys = peer_sub_keys[l].reshape(2 * PEER_HEADS, PEER_KEYS, -1).astype(BF16)
        hb, rank2, e2, cnt, c1 = peer_route(xf, norm_ffn_g[l], peer_wq[l].astype(BF16), keys, t['route'])
        xf = peer_experts(xf, hb, rank2, e2, cnt, c1, peer_u[l].astype(BF16), peer_v[l].T.astype(BF16),
                          t['experts'], t['blocks'])
    return final_norm(xf, final_norm_g, t['final']).reshape(batch, seq, d)
```

```python
import functools
import math

import jax
import jax.numpy as jnp
from jax import lax
from jax.experimental import pallas as pl
from jax.experimental.pallas import tpu as pltpu

F32 = jnp.float32
BF16 = jnp.bfloat16
EPS = 1e-6

CONV_WIDTH = 31
SUBLANES = 8
CONV_HALO = 32
ATTN_HEAD_DIM = 64
SBA_HEADS_PER_STEP = 4
LOG2E = 1.4426950408889634
UNDERFLOW_LOG2 = -150.0
SSM_GROUP = 16
SSM_STATE = 64
MEM_HEADS = 4
PEER_HEADS = 8
PEER_KEYS = 128
PEER_TOPK = 16
RANK_NONE = 99.0
BF16_ROWS = 16
VMEM_LIMIT = 50 * 1024 * 1024


def _params(sem, vmem=VMEM_LIMIT):
    return pltpu.CompilerParams(dimension_semantics=sem, vmem_limit_bytes=vmem)


def _rms(x, g):
    ms = jnp.mean(x * x, axis=-1, keepdims=True)
    return x * lax.rsqrt(ms + EPS) * g


def _gelu(x):
    c = math.sqrt(2.0 / math.pi)
    return 0.5 * x * (1.0 + jnp.tanh(c * (x + 0.044715 * (x * x * x))))


def _sigmoid(x):
    return 1.0 / (1.0 + jnp.exp(-x))


def _norm_matmul_kernel(x_ref, g_ref, w_ref, o_ref):
    h = _rms(x_ref[...], g_ref[...]).astype(BF16)
    o_ref[...] = jnp.dot(h, w_ref[...], preferred_element_type=F32)


def norm_matmul(x, g, w, tm):
    m, d = x.shape
    n = w.shape[1]
    return pl.pallas_call(
        _norm_matmul_kernel,
        grid=(m // tm,),
        in_specs=[pl.BlockSpec((tm, d), lambda i: (i, 0)),
                  pl.BlockSpec((1, d), lambda i: (0, 0)),
                  pl.BlockSpec((d, n), lambda i: (0, 0))],
        out_specs=pl.BlockSpec((tm, n), lambda i: (i, 0)),
        out_shape=jax.ShapeDtypeStruct((m, n), F32),
        compiler_params=_params(("parallel",)),
    )(x, g.reshape(1, d), w)


def _conv_kernel(cur_ref, halo_ref, w_ref, b_ref, lng_ref, lnb_ref, o_ref, ext_ref, phase_ref, *, ts, tiles_per_seq,
                 dc):
    i = pl.program_id(0)
    cur = cur_ref[...]
    halo = halo_ref[...]
    y = cur[:, :dc] * _sigmoid(cur[:, dc:])
    yh = halo[:, :dc] * _sigmoid(halo[:, dc:])
    yh = jnp.where(i % tiles_per_seq == 0, 0.0, yh)
    ext_ref[0:CONV_HALO, :] = yh
    ext_ref[CONV_HALO:, :] = y
    span = ts + CONV_HALO - SUBLANES
    for r in range(1, SUBLANES):
        phase_ref[r - 1, 0:span, :] = ext_ref[r:r + span, :]
    acc = jnp.broadcast_to(b_ref[...], (ts, dc))
    for k in range(CONV_WIDTH):
        off = CONV_HALO - (CONV_WIDTH - 1) + k
        phase, base = off % SUBLANES, off - off % SUBLANES
        window = ext_ref[base:base + ts, :] if phase == 0 else phase_ref[phase - 1, base:base + ts, :]
        acc = acc + w_ref[k:k + 1, :] * window
    mu = jnp.mean(acc, axis=-1, keepdims=True)
    cen = acc - mu
    var = jnp.mean(cen * cen, axis=-1, keepdims=True)
    z = cen * lax.rsqrt(var + EPS) * lng_ref[...] + lnb_ref[...]
    o_ref[...] = z * _sigmoid(z)


def conformer_conv(proj, conv_w, conv_b, ln_g, ln_b, seq, ts):
    m = proj.shape[0]
    dc = conv_w.shape[1]
    kern = functools.partial(_conv_kernel, ts=ts, tiles_per_seq=seq // ts, dc=dc)
    halo_blocks = ts // CONV_HALO
    return pl.pallas_call(
        kern,
        grid=(m // ts,),
        in_specs=[pl.BlockSpec((ts, 2 * dc), lambda i: (i, 0)),
                  pl.BlockSpec((CONV_HALO, 2 * dc), lambda i: (jnp.maximum(i * halo_blocks - 1, 0), 0)),
                  pl.BlockSpec((CONV_WIDTH, dc), lambda i: (0, 0)),
                  pl.BlockSpec((1, dc), lambda i: (0, 0)),
                  pl.BlockSpec((1, dc), lambda i: (0, 0)),
                  pl.BlockSpec((1, dc), lambda i: (0, 0))],
        out_specs=pl.BlockSpec((ts, dc), lambda i: (i, 0)),
        out_shape=jax.ShapeDtypeStruct((m, dc), F32),
        scratch_shapes=[pltpu.VMEM((ts + CONV_HALO, dc), F32),
                        pltpu.VMEM((SUBLANES - 1, ts + CONV_HALO - SUBLANES, dc), F32)],
        compiler_params=_params(("parallel",)),
    )(proj, proj, conv_w, conv_b.reshape(1, dc), ln_g.reshape(1, dc), ln_b.reshape(1, dc))


def _sba_kernel(q_ref, k_ref, v_ref, o_ref, *, tq, scale):
    i = pl.program_id(2)
    dh = ATTN_HEAD_DIM
    row = lax.broadcasted_iota(jnp.int32, (tq, tq), 0)
    col = lax.broadcasted_iota(jnp.int32, (tq, tq), 1)
    causal = col < row
    suffix = (row > col).astype(BF16)

    heads = q_ref.shape[1] // dh
    lanes = [slice(hh * dh, (hh + 1) * dh) for hh in range(heads)]
    qs = [(q_ref[:, ln] * (scale * LOG2E)).astype(BF16) for ln in lanes]

    def block(j, carry, diagonal):
        start = pl.multiple_of(j * tq, tq)
        hs = range(heads)
        kj = [k_ref[pl.ds(start, tq), lanes[h]].astype(BF16) for h in hs]
        vj = [v_ref[pl.ds(start, tq), lanes[h]].astype(BF16) for h in hs]
        z = [lax.dot_general(qs[h], kj[h], (((1,), (1,)), ((), ())), preferred_element_type=F32) for h in hs]
        log_beta, log_stay, later = [], [], []
        for h in hs:
            sp = jnp.log2(1.0 + jnp.exp2(-jnp.abs(z[h])))
            log_beta.append(jnp.minimum(z[h], 0.0) - sp)
            stay = -jnp.maximum(z[h], 0.0) - sp
            log_stay.append(jnp.where(causal, stay, 0.0) if diagonal else stay)
        for h in hs:
            hi = log_stay[h].astype(BF16)
            lo = (log_stay[h] - hi.astype(F32)).astype(BF16)
            later.append(jnp.dot(hi, suffix, preferred_element_type=F32)
                         + jnp.dot(lo, suffix, preferred_element_type=F32))
        out = []
        for h in hs:
            acc, run = carry[h]
            a = jnp.exp2(log_beta[h] + later[h] + run)
            if diagonal:
                a = jnp.where(causal, a, 0.0)
            acc = acc + jnp.dot(a.astype(BF16), vj[h], preferred_element_type=F32)
            run = run + later[h][:, 0:1] + log_stay[h][:, 0:1]
            out.append((acc, run))
        return tuple(out)

    def unfinished(c):
        return jnp.logical_and(c[0] < i, c[1])

    def sweep(c):
        jj, _, carry = c
        carry = block(i - 1 - jj, carry, False)
        top = carry[0][1]
        for h in range(1, heads):
            top = jnp.maximum(top, carry[h][1])
        return jj + 1, jnp.max(top) > UNDERFLOW_LOG2, carry

    carry = tuple((jnp.zeros((tq, dh), F32), jnp.zeros((tq, 1), F32)) for _ in range(heads))
    carry = block(i, carry, True)
    _, _, carry = lax.while_loop(unfinished, sweep, (jnp.int32(0), jnp.bool_(True), carry))
    for hh in range(heads):
        o_ref[:, lanes[hh]] = carry[hh][0]


def stick_breaking_attention(proj, batch, seq, col0, d_attn, tq):
    lane_blk = SBA_HEADS_PER_STEP * ATTN_HEAD_DIM
    hp = d_attn // lane_blk
    qb, kb, vb = col0 // lane_blk, (col0 + d_attn) // lane_blk, (col0 + 2 * d_attn) // lane_blk
    nq = seq // tq
    kern = functools.partial(_sba_kernel, tq=tq, scale=1.0 / math.sqrt(ATTN_HEAD_DIM))
    return pl.pallas_call(
        kern,
        grid=(batch, hp, nq),
        in_specs=[pl.BlockSpec((tq, lane_blk), lambda b, h, i: (b * nq + i, qb + h)),
                  pl.BlockSpec((seq, lane_blk), lambda b, h, i: (b, kb + h)),
                  pl.BlockSpec((seq, lane_blk), lambda b, h, i: (b, vb + h))],
        out_specs=pl.BlockSpec((tq, lane_blk), lambda b, h, i: (b * nq + i, h)),
        out_shape=jax.ShapeDtypeStruct((batch * seq, d_attn), F32),
        compiler_params=_params(("parallel", "parallel", "arbitrary")),
    )(proj, proj, proj)


def _ssm_kernel(u_ref, bblk_ref, pw_ref, cblk_ref, d_ref, gw_ref, gb_ref, o_ref, st_ref, x_ref, *, ts, n):
    c = pl.program_id(1)

    @pl.when(c == 0)
    def _():
        st_ref[...] = jnp.zeros_like(st_ref)

    u = u_ref[...]
    x_ref[...] = jnp.dot(u.astype(BF16), bblk_ref[...], preferred_element_type=F32)

    def group(r, carry):
        sr, si = carry
        rows = pl.ds(pl.multiple_of(r * 8, 8), 8)
        re = x_ref[rows, 0:n]
        im = x_ref[rows, n:2 * n]
        for lvl, d in enumerate((1, 2, 4)):
            lr = pw_ref[lvl, :, 0:n]
            li = pw_ref[lvl, :, n:2 * n]
            pr = pltpu.roll(re, d, 0)
            pi = pltpu.roll(im, d, 0)
            re, im = re + pr * lr - pi * li, im + pr * li + pi * lr
        lr = pw_ref[3, :, 0:n]
        li = pw_ref[3, :, n:2 * n]
        re, im = re + sr * lr - si * li, im + sr * li + si * lr
        x_ref[rows, 0:n] = re
        x_ref[rows, n:2 * n] = im
        return (jnp.broadcast_to(re[7:8, :], (8, n)), jnp.broadcast_to(im[7:8, :], (8, n)))

    sr, si = lax.fori_loop(0, ts // 8, group, (st_ref[:, 0:n], st_ref[:, n:2 * n]))
    st_ref[:, 0:n] = sr
    st_ref[:, n:2 * n] = si

    y = jnp.dot(x_ref[...].astype(BF16), cblk_ref[...], preferred_element_type=F32)
    y = _gelu(y + d_ref[...] * u)
    gate = jnp.dot(y.astype(BF16), gw_ref[...], preferred_element_type=F32) + gb_ref[...]
    o_ref[...] = y * _sigmoid(gate)


def _ssm_tables(lam_re, lam_im, log_dt, b_re, b_im, c_re, c_im):
    g, p = lam_re.shape
    hch = b_re.shape[2]
    n = g * p
    lr = jnp.minimum(lam_re.astype(F32), -1e-4)
    li = lam_im.astype(F32)
    dt = jnp.exp(log_dt.astype(F32))[:, None]

    def lam_pow(steps):
        mag = jnp.exp(lr * dt * steps)
        return mag * jnp.cos(li * dt * steps), mag * jnp.sin(li * dt * steps)

    br, bi = lam_pow(1.0)
    norm = lr * lr + li * li
    cr = ((br - 1.0) * lr + bi * li) / norm
    ci = (bi * lr - (br - 1.0) * li) / norm
    bre, bim = b_re.astype(F32), b_im.astype(F32)
    bbar_re = cr[..., None] * bre - ci[..., None] * bim
    bbar_im = cr[..., None] * bim + ci[..., None] * bre
    eye = jnp.eye(g, dtype=F32)
    bb_re = jnp.einsum('gph,gk->ghkp', bbar_re, eye).reshape(g * hch, n)
    bb_im = jnp.einsum('gph,gk->ghkp', bbar_im, eye).reshape(g * hch, n)
    bblk = jnp.concatenate([bb_re, bb_im], axis=1).astype(BF16)
    cc_re = jnp.einsum('ghp,gk->kpgh', c_re.astype(F32), eye).reshape(n, g * hch)
    cc_im = jnp.einsum('ghp,gk->kpgh', c_im.astype(F32), eye).reshape(n, g * hch)
    cblk = jnp.concatenate([cc_re, -cc_im], axis=0).astype(BF16)
    t = jnp.arange(8)
    tabs = []
    for d in (1, 2, 4):
        pr, pi = lam_pow(float(d))
        keep = (t >= d)[:, None]
        tabs.append(jnp.concatenate([jnp.where(keep, pr.reshape(1, n), 0.0),
                                     jnp.where(keep, pi.reshape(1, n), 0.0)], axis=1))
    rows = [lam_pow(float(k + 1)) for k in range(8)]
    tabs.append(jnp.concatenate([jnp.stack([r[0].reshape(n) for r in rows]),
                                 jnp.stack([r[1].reshape(n) for r in rows])], axis=1))
    return bblk, cblk, jnp.stack(tabs).astype(F32)


def s5_ssm(proj, batch, seq, col0, bblk, cblk, pw, d_skip, glu_w, glu_b, ts):
    ds = bblk.shape[0]
    n = bblk.shape[1] // 2
    nt = seq // ts
    kern = functools.partial(_ssm_kernel, ts=ts, n=n)
    cb = col0 // ds
    return pl.pallas_call(
        kern,
        grid=(batch, nt),
        in_specs=[pl.BlockSpec((ts, ds), lambda b, c: (b * nt + c, cb)),
                  pl.BlockSpec((ds, 2 * n), lambda b, c: (0, 0)),
                  pl.BlockSpec((4, 8, 2 * n), lambda b, c: (0, 0, 0)),
                  pl.BlockSpec((2 * n, ds), lambda b, c: (0, 0)),
                  pl.BlockSpec((1, ds), lambda b, c: (0, 0)),
                  pl.BlockSpec((ds, ds), lambda b, c: (0, 0)),
                  pl.BlockSpec((1, ds), lambda b, c: (0, 0))],
        out_specs=pl.BlockSpec((ts, ds), lambda b, c: (b * nt + c, 0)),
        out_shape=jax.ShapeDtypeStruct((batch * seq, ds), F32),
        scratch_shapes=[pltpu.VMEM((8, 2 * n), F32), pltpu.VMEM((ts, 2 * n), F32)],
        compiler_params=_params(("parallel", "arbitrary")),
    )(proj, bblk, pw, cblk, d_skip.reshape(1, ds), glu_w.astype(BF16), glu_b.reshape(1, ds))


def _mix_mem_kernel(x_ref, yc_ref, ya_ref, ys_ref, gg_ref, wout_ref, g_ref, wq_ref, kv_ref, wo_ref, o_ref, *, heads):
    x = x_ref[...]
    off = 0
    for y_ref in (yc_ref, ya_ref, ys_ref):
        width = y_ref.shape[1]
        h = _rms(y_ref[...], gg_ref[:, off:off + width]).astype(BF16)
        x = x + jnp.dot(h, wout_ref[off:off + width, :], preferred_element_type=F32)
        off += width
    d = x.shape[1]
    dh = d // heads
    h = _rms(x, g_ref[...]).astype(BF16)
    q = jnp.dot(h, wq_ref[...], preferred_element_type=F32) * (1.0 / math.sqrt(dh))
    outs = []
    for hd in range(heads):
        qh = q[:, hd * dh:(hd + 1) * dh].astype(BF16)
        kh = kv_ref[:, hd * dh:(hd + 1) * dh].astype(BF16)
        vh = kv_ref[:, d + hd * dh:d + (hd + 1) * dh].astype(BF16)
        s = lax.dot_general(qh, kh, (((1,), (1,)), ((), ())), preferred_element_type=F32)
        e = jnp.exp(s - jnp.max(s, axis=-1, keepdims=True))
        p = e / jnp.sum(e, axis=-1, keepdims=True)
        outs.append(jnp.dot(p.astype(BF16), vh, preferred_element_type=F32).astype(BF16))
    o = jnp.concatenate(outs, axis=-1)
    o_ref[...] = x + jnp.dot(o, wo_ref[...], preferred_element_type=F32)


def mix_mem(x, yc, ya, ys, g_grp, w_out, g_mem, wq, memkv, wo, seq, n_mem, tm):
    m, d = x.shape
    dc, da, dsm = yc.shape[1], ya.shape[1], ys.shape[1]
    tiles_per_seq = seq // tm
    kern = functools.partial(_mix_mem_kernel, heads=MEM_HEADS)
    row_blk = lambda i: (i, 0)
    whole = lambda i: (0, 0)
    return pl.pallas_call(
        kern,
        grid=(m // tm,),
        in_specs=[pl.BlockSpec((tm, d), row_blk),
                  pl.BlockSpec((tm, dc), row_blk),
                  pl.BlockSpec((tm, da), row_blk),
                  pl.BlockSpec((tm, dsm), row_blk),
                  pl.BlockSpec((1, dc + da + dsm), whole),
                  pl.BlockSpec((dc + da + dsm, d), whole),
                  pl.BlockSpec((1, d), whole),
                  pl.BlockSpec((d, d), whole),
                  pl.BlockSpec((n_mem, 2 * d), lambda i: (i // tiles_per_seq, 0)),
                  pl.BlockSpec((d, d), whole)],
        out_specs=pl.BlockSpec((tm, d), row_blk),
        out_shape=jax.ShapeDtypeStruct((m, d), F32),
        compiler_params=_params(("parallel",)),
    )(x, yc, ya, ys, g_grp.reshape(1, -1), w_out, g_mem.reshape(1, d), wq, memkv, wo)


def _top16(scs, want_rank):
    ranks = [jnp.full(sc.shape, RANK_NONE, F32) if w else None for sc, w in zip(scs, want_rank)]
    vals = [[] for _ in scs]
    scs = list(scs)
    for r in range(PEER_TOPK):
        for n in range(len(scs)):
            m = jnp.max(scs[n], axis=0, keepdims=True)
            sel = scs[n] == m
            if want_rank[n]:
                ranks[n] = jnp.where(sel, float(r), ranks[n])
            scs[n] = jnp.where(sel, -jnp.inf, scs[n])
            vals[n].append(m)
    return vals, ranks


def _peer_route_kernel(x_ref, g_ref, wq_ref, keys_ref, hb_ref, rank2_ref, e2_ref, cnt_ref, c1_ref,
                       q_scr, top_scr, *, tp):
    nk = PEER_KEYS
    h = _rms(x_ref[...], g_ref[...]).astype(BF16)
    hb_ref[...] = h
    q = jnp.dot(h, wq_ref[...], preferred_element_type=F32)
    for c in range(2 * PEER_HEADS):
        q_scr[c] = q[:, c * nk:(c + 1) * nk].astype(BF16)

    def head(hd, _):
        nt = (((1,), (1,)), ((), ()))
        s1 = lax.dot_general(keys_ref[2 * hd], q_scr[2 * hd], nt, preferred_element_type=F32)
        s2 = lax.dot_general(keys_ref[2 * hd + 1], q_scr[2 * hd + 1], nt, preferred_element_type=F32)
        (va, vb), (_, rank2) = _top16([s1, s2], [False, True])
        for r in range(PEER_TOPK):
            top_scr[0, r:r + 1, :] = va[r]
            top_scr[1, r:r + 1, :] = vb[r]
        a = top_scr[0]
        b = top_scr[1]
        cands = [a[0:1, :] + b]
        for j in range(1, 8):
            cands.append(a[j:j + 1, :] + b[0:8, :])
        cands.append(a[8:16, :] + b[0:1, :])
        (best,), _ = _top16([jnp.concatenate(cands, axis=0)], [False])
        thr = best[PEER_TOPK - 1]
        zsum = jnp.zeros_like(thr)
        for r in range(PEER_TOPK):
            zsum = zsum + jnp.exp(best[r] - best[0])
        cnt = jnp.zeros((nk, tp), F32)
        for j in range(PEER_TOPK):
            cj = jnp.sum(((a[j:j + 1, :] + b) >= thr).astype(F32), axis=0, keepdims=True)
            cnt = jnp.where(s1 == a[j:j + 1, :], cj, cnt)
        cnt_ref[hd] = cnt
        c1_ref[hd] = jnp.exp(s1 - a[0:1, :]) / zsum
        packed = (nk // BF16_ROWS, BF16_ROWS, tp)
        rank2_ref[hd] = rank2.reshape(packed).astype(BF16)
        e2_ref[hd] = jnp.exp(s2 - b[0:1, :]).reshape(packed).astype(BF16)
        return 0

    lax.fori_loop(0, PEER_HEADS, head, 0)


def peer_route(x, g, wq, keys, tp):
    m, d = x.shape
    nq = wq.shape[1]
    nk = PEER_KEYS
    kern = functools.partial(_peer_route_kernel, tp=tp)
    tok_blk = lambda i: (0, 0, i)
    packed_blk = lambda i: (0, 0, 0, i)
    packed = (PEER_HEADS, nk // BF16_ROWS, BF16_ROWS)
    return pl.pallas_call(
        kern,
        grid=(m // tp,),
        in_specs=[pl.BlockSpec((tp, d), lambda i: (i, 0)),
                  pl.BlockSpec((1, d), lambda i: (0, 0)),
                  pl.BlockSpec((d, nq), lambda i: (0, 0)),
                  pl.BlockSpec((2 * PEER_HEADS, nk, nk), lambda i: (0, 0, 0))],
        out_specs=[pl.BlockSpec((tp, d), lambda i: (i, 0)),
                   pl.BlockSpec(packed + (tp,), packed_blk),
                   pl.BlockSpec(packed + (tp,), packed_blk),
                   pl.BlockSpec((PEER_HEADS, nk, tp), tok_blk),
                   pl.BlockSpec((PEER_HEADS, nk, tp), tok_blk)],
        out_shape=[jax.ShapeDtypeStruct((m, d), BF16),
                   jax.ShapeDtypeStruct(packed + (m,), BF16),
                   jax.ShapeDtypeStruct(packed + (m,), BF16),
                   jax.ShapeDtypeStruct((PEER_HEADS, nk, m), F32),
                   jax.ShapeDtypeStruct((PEER_HEADS, nk, m), F32)],
        scratch_shapes=[pltpu.VMEM((2 * PEER_HEADS, tp, nk), BF16),
                        pltpu.VMEM((2, PEER_TOPK, tp), F32)],
        compiler_params=_params(("parallel",)),
    )(x, g.reshape(1, d), wq, keys)


def _peer_experts_kernel(x_ref, hb_ref, rank2_ref, e2_ref, cnt_ref, c1_ref, u_ref, vt_ref, fg_ref, o_ref,
                         acc_ref, hw_ref, *, blocks, final_norm):
    e = pl.program_id(1)
    nk = PEER_KEYS
    tt = hb_ref.shape[0]
    tiles = nk // BF16_ROWS

    @pl.when(e == 0)
    def _():
        acc_ref[...] = jnp.zeros_like(acc_ref)

    pre = lax.dot_general(u_ref[...], hb_ref[...], (((1,), (1,)), ((), ())), preferred_element_type=F32)
    for j in range(blocks):
        w = None
        for hd in range(PEER_HEADS):
            cnt = jnp.broadcast_to(cnt_ref[hd, j:j + 1, :], (BF16_ROWS, tt)).astype(BF16)
            c1 = jnp.broadcast_to(c1_ref[hd, j:j + 1, :], (BF16_ROWS, tt)).astype(BF16)
            term = jnp.where(rank2_ref[hd] < cnt[None], e2_ref[hd], jnp.zeros((), BF16)) * c1[None]
            w = term if w is None else w + term
        act = _gelu(pre[j * nk:(j + 1) * nk, :].reshape(tiles, BF16_ROWS, tt).astype(BF16))
        hw_ref[j * tiles:(j + 1) * tiles] = act * w
    hw = hw_ref[...].reshape(blocks * nk, tt)
    acc_ref[...] += jnp.dot(vt_ref[...], hw, preferred_element_type=F32)

    @pl.when(e == pl.num_programs(1) - 1)
    def _():
        out = x_ref[...] + acc_ref[...].T
        o_ref[...] = _rms(out, fg_ref[...]) if final_norm else out


def peer_experts(x, hb, rank2, e2, cnt, c1, u_tab, vt_tab, final_g, final_norm, tt, blocks):
    m, d = x.shape
    n_exp = u_tab.shape[0]
    nk = PEER_KEYS
    te = blocks * nk
    kern = functools.partial(_peer_experts_kernel, blocks=blocks, final_norm=final_norm)
    packed = (PEER_HEADS, nk // BF16_ROWS, BF16_ROWS, tt)
    return pl.pallas_call(
        kern,
        grid=(m // tt, n_exp // te),
        in_specs=[pl.BlockSpec((tt, d), lambda t, e: (t, 0)),
                  pl.BlockSpec((tt, d), lambda t, e: (t, 0)),
                  pl.BlockSpec(packed, lambda t, e: (0, 0, 0, t)),
                  pl.BlockSpec(packed, lambda t, e: (0, 0, 0, t)),
                  pl.BlockSpec((PEER_HEADS, blocks, tt), lambda t, e: (0, e, t)),
                  pl.BlockSpec((PEER_HEADS, blocks, tt), lambda t, e: (0, e, t)),
                  pl.BlockSpec((te, d), lambda t, e: (e, 0)),
                  pl.BlockSpec((d, te), lambda t, e: (0, e)),
                  pl.BlockSpec((1, d), lambda t, e: (0, 0))],
        out_specs=pl.BlockSpec((tt, d), lambda t, e: (t, 0)),
        out_shape=jax.ShapeDtypeStruct((m, d), F32),
        scratch_shapes=[pltpu.VMEM((d, tt), F32), pltpu.VMEM((te // BF16_ROWS, BF16_ROWS, tt), BF16)],
        compiler_params=_params(("parallel", "arbitrary")),
    )(x, hb, rank2, e2, cnt, c1, u_tab, vt_tab, final_g.reshape(1, d))


def _tiles(seq, n_mem):
    return dict(
        proj=min(512, seq), conv=min(512, seq), sba=min(256, seq), ssm=min(512, seq), mix=min(512, seq),
        memkv=min(256, n_mem), route=min(512, seq), experts=min(1024, seq), blocks=8)


def kernel(x, mem, norm_mix_g, w_in, conv_w, conv_b, conv_ln_g, conv_ln_b, ssm_lambda_re, ssm_lambda_im, ssm_log_dt, ssm_b_re, ssm_b_im, ssm_c_re, ssm_c_im, ssm_d, ssm_glu_w, ssm_glu_b, grp_norm_g, w_out, norm_mem_g, mem_norm_g, w_mq, w_mkv, w_mo, norm_ffn_g, peer_wq, peer_sub_keys, peer_u, peer_v, final_norm_g):
    batch, seq, d = x.shape
    n_mem = mem.shape[1]
    depth = w_in.shape[0]
    d_conv = conv_w.shape[2]
    d_ssm = ssm_d.shape[1]
    d_attn = (w_in.shape[2] - 2 * d_conv - d_ssm) // 3
    t = _tiles(seq, n_mem)

    xf = x.reshape(batch * seq, d)
    memf = mem.reshape(batch * n_mem, d)
    for l in range(depth):
        proj = norm_matmul(xf, norm_mix_g[l], w_in[l].astype(BF16), t['proj'])
        yc = conformer_conv(proj, conv_w[l], conv_b[l], conv_ln_g[l], conv_ln_b[l], seq, t['conv'])
        ya = stick_breaking_attention(proj, batch, seq, 2 * d_conv, d_attn, t['sba'])
        bblk, cblk, pw = _ssm_tables(ssm_lambda_re[l], ssm_lambda_im[l], ssm_log_dt[l], ssm_b_re[l], ssm_b_im[l],
                                     ssm_c_re[l], ssm_c_im[l])
        ys = s5_ssm(proj, batch, seq, 2 * d_conv + 3 * d_attn, bblk, cblk, pw, ssm_d[l], ssm_glu_w[l], ssm_glu_b[l],
                    t['ssm'])
        memkv = norm_matmul(memf, mem_norm_g[l], w_mkv[l].astype(BF16), t['memkv'])
        xf = mix_mem(xf, yc, ya, ys, grp_norm_g[l], w_out[l].astype(BF16), norm_mem_g[l], w_mq[l].astype(BF16), memkv,
                     w_mo[l].astype(BF16), seq, n_mem, t['mix'])

        keys = peer_sub_keys[l].reshape(2 * PEER_HEADS, PEER_KEYS, -1).astype(BF16)
        hb, rank2, e2, cnt, c1 = peer_route(xf, norm_ffn_g[l], peer_wq[l].astype(BF16), keys, t['route'])
        xf = peer_experts(xf, hb, rank2, e2, cnt, c1, peer_u[l].astype(BF16), peer_v[l].T.astype(BF16),
                          final_norm_g, l == depth - 1, t['experts'], t['blocks'])
    return xf.reshape(batch, seq, d)
```

```python
import functools
import math

import jax
import jax.numpy as jnp
from jax import lax
from jax.experimental import pallas as pl
from jax.experimental.pallas import tpu as pltpu

F32 = jnp.float32
BF16 = jnp.bfloat16
EPS = 1e-6

CONV_WIDTH = 31
SUBLANES = 8
CONV_HALO = 32
ATTN_HEAD_DIM = 64
SBA_HEADS_PER_STEP = 4
LOG2E = 1.4426950408889634
UNDERFLOW_LOG2 = -160.0
SSM_GROUP = 16
SSM_STATE = 64
MEM_HEADS = 4
PEER_HEADS = 8
PEER_KEYS = 128
PEER_TOPK = 16
RANK_NONE = 99.0
BF16_ROWS = 16
VMEM_LIMIT = 50 * 1024 * 1024


def _params(sem, vmem=VMEM_LIMIT):
    return pltpu.CompilerParams(dimension_semantics=sem, vmem_limit_bytes=vmem)


def _rms(x, g):
    ms = jnp.mean(x * x, axis=-1, keepdims=True)
    return x * lax.rsqrt(ms + EPS) * g


def _gelu(x):
    c = math.sqrt(2.0 / math.pi)
    return 0.5 * x * (1.0 + jnp.tanh(c * (x + 0.044715 * (x * x * x))))


def _sigmoid(x):
    return 1.0 / (1.0 + jnp.exp(-x))


def _norm_matmul_kernel(x_ref, g_ref, w_ref, o_ref):
    h = _rms(x_ref[...], g_ref[...]).astype(BF16)
    o_ref[...] = jnp.dot(h, w_ref[...], preferred_element_type=F32)


def norm_matmul(x, g, w, tm):
    m, d = x.shape
    n = w.shape[1]
    return pl.pallas_call(
        _norm_matmul_kernel,
        grid=(m // tm,),
        in_specs=[pl.BlockSpec((tm, d), lambda i: (i, 0)),
                  pl.BlockSpec((1, d), lambda i: (0, 0)),
                  pl.BlockSpec((d, n), lambda i: (0, 0))],
        out_specs=pl.BlockSpec((tm, n), lambda i: (i, 0)),
        out_shape=jax.ShapeDtypeStruct((m, n), F32),
        compiler_params=_params(("parallel",)),
    )(x, g.reshape(1, d), w)


def _conv_kernel(cur_ref, halo_ref, w_ref, b_ref, lng_ref, lnb_ref, o_ref, ext_ref, phase_ref, *, ts, tiles_per_seq,
                 dc):
    i = pl.program_id(0)
    cur = cur_ref[...]
    halo = halo_ref[...]
    y = cur[:, :dc] * _sigmoid(cur[:, dc:])
    yh = halo[:, :dc] * _sigmoid(halo[:, dc:])
    yh = jnp.where(i % tiles_per_seq == 0, 0.0, yh)
    ext_ref[0:CONV_HALO, :] = yh
    ext_ref[CONV_HALO:, :] = y
    span = ts + CONV_HALO - SUBLANES
    for r in range(1, SUBLANES):
        phase_ref[r - 1, 0:span, :] = ext_ref[r:r + span, :]
    acc = jnp.broadcast_to(b_ref[...], (ts, dc))
    for k in range(CONV_WIDTH):
        off = CONV_HALO - (CONV_WIDTH - 1) + k
        phase, base = off % SUBLANES, off - off % SUBLANES
        window = ext_ref[base:base + ts, :] if phase == 0 else phase_ref[phase - 1, base:base + ts, :]
        acc = acc + w_ref[k:k + 1, :] * window
    mu = jnp.mean(acc, axis=-1, keepdims=True)
    cen = acc - mu
    var = jnp.mean(cen * cen, axis=-1, keepdims=True)
    z = cen * lax.rsqrt(var + EPS) * lng_ref[...] + lnb_ref[...]
    o_ref[...] = z * _sigmoid(z)


def conformer_conv(proj, conv_w, conv_b, ln_g, ln_b, seq, ts):
    m = proj.shape[0]
    dc = conv_w.shape[1]
    kern = functools.partial(_conv_kernel, ts=ts, tiles_per_seq=seq // ts, dc=dc)
    halo_blocks = ts // CONV_HALO
    return pl.pallas_call(
        kern,
        grid=(m // ts,),
        in_specs=[pl.BlockSpec((ts, 2 * dc), lambda i: (i, 0)),
                  pl.BlockSpec((CONV_HALO, 2 * dc), lambda i: (jnp.maximum(i * halo_blocks - 1, 0), 0)),
                  pl.BlockSpec((CONV_WIDTH, dc), lambda i: (0, 0)),
                  pl.BlockSpec((1, dc), lambda i: (0, 0)),
                  pl.BlockSpec((1, dc), lambda i: (0, 0)),
                  pl.BlockSpec((1, dc), lambda i: (0, 0))],
        out_specs=pl.BlockSpec((ts, dc), lambda i: (i, 0)),
        out_shape=jax.ShapeDtypeStruct((m, dc), F32),
        scratch_shapes=[pltpu.VMEM((ts + CONV_HALO, dc), F32),
                        pltpu.VMEM((SUBLANES - 1, ts + CONV_HALO - SUBLANES, dc), F32)],
        compiler_params=_params(("parallel",)),
    )(proj, proj, conv_w, conv_b.reshape(1, dc), ln_g.reshape(1, dc), ln_b.reshape(1, dc))


def _sba_kernel(q_ref, k_ref, v_ref, o_ref, *, tq, scale):
    i = pl.program_id(2)
    dh = ATTN_HEAD_DIM
    row = lax.broadcasted_iota(jnp.int32, (tq, tq), 0)
    col = lax.broadcasted_iota(jnp.int32, (tq, tq), 1)
    causal = col < row
    suffix = (row > col).astype(BF16)

    heads = q_ref.shape[1] // dh
    lanes = [slice(hh * dh, (hh + 1) * dh) for hh in range(heads)]
    qs = [(q_ref[:, ln] * (scale * LOG2E)).astype(BF16) for ln in lanes]

    def block(j, carry, diagonal):
        start = pl.multiple_of(j * tq, tq)
        hs = range(heads)
        kj = [k_ref[pl.ds(start, tq), lanes[h]].astype(BF16) for h in hs]
        vj = [v_ref[pl.ds(start, tq), lanes[h]].astype(BF16) for h in hs]
        z = [lax.dot_general(qs[h], kj[h], (((1,), (1,)), ((), ())), preferred_element_type=F32) for h in hs]
        log_beta, log_stay, later = [], [], []
        for h in hs:
            sp = jnp.log2(1.0 + jnp.exp2(-jnp.abs(z[h])))
            log_beta.append(jnp.minimum(z[h], 0.0) - sp)
            stay = -jnp.maximum(z[h], 0.0) - sp
            log_stay.append(jnp.where(causal, stay, 0.0) if diagonal else stay)
        for h in hs:
            later.append(jnp.dot(log_stay[h].astype(BF16), suffix, preferred_element_type=F32))
        out = []
        for h in hs:
            acc, run = carry[h]
            a = jnp.exp2(log_beta[h] + later[h] + run)
            if diagonal:
                a = jnp.where(causal, a, 0.0)
            acc = acc + jnp.dot(a.astype(BF16), vj[h], preferred_element_type=F32)
            run = run + later[h][:, 0:1] + log_stay[h][:, 0:1]
            out.append((acc, run))
        return tuple(out)

    def unfinished(c):
        return jnp.logical_and(c[0] < i, c[1])

    def sweep(c):
        jj, _, carry = c
        carry = block(i - 1 - jj, carry, False)
        top = carry[0][1]
        for h in range(1, heads):
            top = jnp.maximum(top, carry[h][1])
        return jj + 1, jnp.max(top) > UNDERFLOW_LOG2, carry

    carry = tuple((jnp.zeros((tq, dh), F32), jnp.zeros((tq, 1), F32)) for _ in range(heads))
    carry = block(i, carry, True)
    _, _, carry = lax.while_loop(unfinished, sweep, (jnp.int32(0), jnp.bool_(True), carry))
    for hh in range(heads):
        o_ref[:, lanes[hh]] = carry[hh][0]


def stick_breaking_attention(proj, batch, seq, col0, d_attn, tq):
    lane_blk = SBA_HEADS_PER_STEP * ATTN_HEAD_DIM
    hp = d_attn // lane_blk
    qb, kb, vb = col0 // lane_blk, (col0 + d_attn) // lane_blk, (col0 + 2 * d_attn) // lane_blk
    nq = seq // tq
    kern = functools.partial(_sba_kernel, tq=tq, scale=1.0 / math.sqrt(ATTN_HEAD_DIM))
    return pl.pallas_call(
        kern,
        grid=(batch, hp, nq),
        in_specs=[pl.BlockSpec((tq, lane_blk), lambda b, h, i: (b * nq + i, qb + h)),
                  pl.BlockSpec((seq, lane_blk), lambda b, h, i: (b, kb + h)),
                  pl.BlockSpec((seq, lane_blk), lambda b, h, i: (b, vb + h))],
        out_specs=pl.BlockSpec((tq, lane_blk), lambda b, h, i: (b * nq + i, h)),
        out_shape=jax.ShapeDtypeStruct((batch * seq, d_attn), F32),
        compiler_params=_params(("parallel", "parallel", "arbitrary")),
    )(proj, proj, proj)


def _ssm_kernel(u_ref, bblk_ref, pw_ref, cblk_ref, d_ref, gw_ref, gb_ref, o_ref, st_ref, x_ref, *, ts, n):
    c = pl.program_id(1)

    @pl.when(c == 0)
    def _():
        st_ref[...] = jnp.zeros_like(st_ref)

    u = u_ref[...]
    x_ref[...] = jnp.dot(u.astype(BF16), bblk_ref[...], preferred_element_type=F32)

    def group(r, carry):
        sr, si = carry
        rows = pl.ds(pl.multiple_of(r * 8, 8), 8)
        re = x_ref[rows, 0:n]
        im = x_ref[rows, n:2 * n]
        for lvl, d in enumerate((1, 2, 4)):
            lr = pw_ref[lvl, :, 0:n]
            li = pw_ref[lvl, :, n:2 * n]
            pr = pltpu.roll(re, d, 0)
            pi = pltpu.roll(im, d, 0)
            re, im = re + pr * lr - pi * li, im + pr * li + pi * lr
        lr = pw_ref[3, :, 0:n]
        li = pw_ref[3, :, n:2 * n]
        re, im = re + sr * lr - si * li, im + sr * li + si * lr
        x_ref[rows, 0:n] = re
        x_ref[rows, n:2 * n] = im
        return (jnp.broadcast_to(re[7:8, :], (8, n)), jnp.broadcast_to(im[7:8, :], (8, n)))

    sr, si = lax.fori_loop(0, ts // 8, group, (st_ref[:, 0:n], st_ref[:, n:2 * n]))
    st_ref[:, 0:n] = sr
    st_ref[:, n:2 * n] = si

    y = jnp.dot(x_ref[...].astype(BF16), cblk_ref[...], preferred_element_type=F32)
    y = _gelu(y + d_ref[...] * u)
    gate = jnp.dot(y.astype(BF16), gw_ref[...], preferred_element_type=F32) + gb_ref[...]
    o_ref[...] = y * _sigmoid(gate)


def _ssm_tables(lam_re, lam_im, log_dt, b_re, b_im, c_re, c_im):
    g, p = lam_re.shape
    hch = b_re.shape[2]
    n = g * p
    lr = jnp.minimum(lam_re.astype(F32), -1e-4)
    li = lam_im.astype(F32)
    dt = jnp.exp(log_dt.astype(F32))[:, None]

    def lam_pow(steps):
        mag = jnp.exp(lr * dt * steps)
        return mag * jnp.cos(li * dt * steps), mag * jnp.sin(li * dt * steps)

    br, bi = lam_pow(1.0)
    norm = lr * lr + li * li
    cr = ((br - 1.0) * lr + bi * li) / norm
    ci = (bi * lr - (br - 1.0) * li) / norm
    bre, bim = b_re.astype(F32), b_im.astype(F32)
    bbar_re = cr[..., None] * bre - ci[..., None] * bim
    bbar_im = cr[..., None] * bim + ci[..., None] * bre
    eye = jnp.eye(g, dtype=F32)
    bb_re = jnp.einsum('gph,gk->ghkp', bbar_re, eye).reshape(g * hch, n)
    bb_im = jnp.einsum('gph,gk->ghkp', bbar_im, eye).reshape(g * hch, n)
    bblk = jnp.concatenate([bb_re, bb_im], axis=1).astype(BF16)
    cc_re = jnp.einsum('ghp,gk->kpgh', c_re.astype(F32), eye).reshape(n, g * hch)
    cc_im = jnp.einsum('ghp,gk->kpgh', c_im.astype(F32), eye).reshape(n, g * hch)
    cblk = jnp.concatenate([cc_re, -cc_im], axis=0).astype(BF16)
    t = jnp.arange(8)
    tabs = []
    for d in (1, 2, 4):
        pr, pi = lam_pow(float(d))
        keep = (t >= d)[:, None]
        tabs.append(jnp.concatenate([jnp.where(keep, pr.reshape(1, n), 0.0),
                                     jnp.where(keep, pi.reshape(1, n), 0.0)], axis=1))
    rows = [lam_pow(float(k + 1)) for k in range(8)]
    tabs.append(jnp.concatenate([jnp.stack([r[0].reshape(n) for r in rows]),
                                 jnp.stack([r[1].reshape(n) for r in rows])], axis=1))
    return bblk, cblk, jnp.stack(tabs).astype(F32)


def s5_ssm(proj, batch, seq, col0, bblk, cblk, pw, d_skip, glu_w, glu_b, ts):
    ds = bblk.shape[0]
    n = bblk.shape[1] // 2
    nt = seq // ts
    kern = functools.partial(_ssm_kernel, ts=ts, n=n)
    cb = col0 // ds
    return pl.pallas_call(
        kern,
        grid=(batch, nt),
        in_specs=[pl.BlockSpec((ts, ds), lambda b, c: (b * nt + c, cb)),
                  pl.BlockSpec((ds, 2 * n), lambda b, c: (0, 0)),
                  pl.BlockSpec((4, 8, 2 * n), lambda b, c: (0, 0, 0)),
                  pl.BlockSpec((2 * n, ds), lambda b, c: (0, 0)),
                  pl.BlockSpec((1, ds), lambda b, c: (0, 0)),
                  pl.BlockSpec((ds, ds), lambda b, c: (0, 0)),
                  pl.BlockSpec((1, ds), lambda b, c: (0, 0))],
        out_specs=pl.BlockSpec((ts, ds), lambda b, c: (b * nt + c, 0)),
        out_shape=jax.ShapeDtypeStruct((batch * seq, ds), F32),
        scratch_shapes=[pltpu.VMEM((8, 2 * n), F32), pltpu.VMEM((ts, 2 * n), F32)],
        compiler_params=_params(("parallel", "arbitrary")),
    )(proj, bblk, pw, cblk, d_skip.reshape(1, ds), glu_w.astype(BF16), glu_b.reshape(1, ds))


def _mix_mem_kernel(x_ref, yc_ref, ya_ref, ys_ref, gg_ref, wout_ref, g_ref, wq_ref, kv_ref, wo_ref, o_ref, *, heads):
    x = x_ref[...]
    off = 0
    for y_ref in (yc_ref, ya_ref, ys_ref):
        width = y_ref.shape[1]
        h = _rms(y_ref[...], gg_ref[:, off:off + width]).astype(BF16)
        x = x + jnp.dot(h, wout_ref[off:off + width, :], preferred_element_type=F32)
        off += width
    d = x.shape[1]
    dh = d // heads
    h = _rms(x, g_ref[...]).astype(BF16)
    q = jnp.dot(h, wq_ref[...], preferred_element_type=F32) * (1.0 / math.sqrt(dh))
    outs = []
    for hd in range(heads):
        qh = q[:, hd * dh:(hd + 1) * dh].astype(BF16)
        kh = kv_ref[:, hd * dh:(hd + 1) * dh].astype(BF16)
        vh = kv_ref[:, d + hd * dh:d + (hd + 1) * dh].astype(BF16)
        s = lax.dot_general(qh, kh, (((1,), (1,)), ((), ())), preferred_element_type=F32)
        e = jnp.exp(s - jnp.max(s, axis=-1, keepdims=True))
        p = e / jnp.sum(e, axis=-1, keepdims=True)
        outs.append(jnp.dot(p.astype(BF16), vh, preferred_element_type=F32).astype(BF16))
    o = jnp.concatenate(outs, axis=-1)
    o_ref[...] = x + jnp.dot(o, wo_ref[...], preferred_element_type=F32)


def mix_mem(x, yc, ya, ys, g_grp, w_out, g_mem, wq, memkv, wo, seq, n_mem, tm):
    m, d = x.shape
    dc, da, dsm = yc.shape[1], ya.shape[1], ys.shape[1]
    tiles_per_seq = seq // tm
    kern = functools.partial(_mix_mem_kernel, heads=MEM_HEADS)
    row_blk = lambda i: (i, 0)
    whole = lambda i: (0, 0)
    return pl.pallas_call(
        kern,
        grid=(m // tm,),
        in_specs=[pl.BlockSpec((tm, d), row_blk),
                  pl.BlockSpec((tm, dc), row_blk),
                  pl.BlockSpec((tm, da), row_blk),
                  pl.BlockSpec((tm, dsm), row_blk),
                  pl.BlockSpec((1, dc + da + dsm), whole),
                  pl.BlockSpec((dc + da + dsm, d), whole),
                  pl.BlockSpec((1, d), whole),
                  pl.BlockSpec((d, d), whole),
                  pl.BlockSpec((n_mem, 2 * d), lambda i: (i // tiles_per_seq, 0)),
                  pl.BlockSpec((d, d), whole)],
        out_specs=pl.BlockSpec((tm, d), row_blk),
        out_shape=jax.ShapeDtypeStruct((m, d), F32),
        compiler_params=_params(("parallel",)),
    )(x, yc, ya, ys, g_grp.reshape(1, -1), w_out, g_mem.reshape(1, d), wq, memkv, wo)


def _top16(scs, want_rank):
    ranks = [jnp.full(sc.shape, RANK_NONE, F32) if w else None for sc, w in zip(scs, want_rank)]
    vals = [[] for _ in scs]
    scs = list(scs)
    for r in range(PEER_TOPK):
        for n in range(len(scs)):
            m = jnp.max(scs[n], axis=0, keepdims=True)
            sel = scs[n] == m
            if want_rank[n]:
                ranks[n] = jnp.where(sel, float(r), ranks[n])
            scs[n] = jnp.where(sel, -jnp.inf, scs[n])
            vals[n].append(m)
    return vals, ranks


def _peer_route_kernel(x_ref, g_ref, wq_ref, keys_ref, hb_ref, rank2_ref, e2_ref, cnt_ref, c1_ref,
                       q_scr, top_scr, *, tp):
    nk = PEER_KEYS
    h = _rms(x_ref[...], g_ref[...]).astype(BF16)
    hb_ref[...] = h
    q = jnp.dot(h, wq_ref[...], preferred_element_type=F32)
    for c in range(2 * PEER_HEADS):
        q_scr[c] = q[:, c * nk:(c + 1) * nk].astype(BF16)

    def head(hd, _):
        nt = (((1,), (1,)), ((), ()))
        s1 = lax.dot_general(keys_ref[2 * hd], q_scr[2 * hd], nt, preferred_element_type=F32)
        s2 = lax.dot_general(keys_ref[2 * hd + 1], q_scr[2 * hd + 1], nt, preferred_element_type=F32)
        (va, vb), (_, rank2) = _top16([s1, s2], [False, True])
        for r in range(PEER_TOPK):
            top_scr[0, r:r + 1, :] = va[r]
            top_scr[1, r:r + 1, :] = vb[r]
        a = top_scr[0]
        b = top_scr[1]
        cands = [a[0:1, :] + b]
        for j in range(1, 8):
            cands.append(a[j:j + 1, :] + b[0:8, :])
        cands.append(a[8:16, :] + b[0:1, :])
        (best,), _ = _top16([jnp.concatenate(cands, axis=0)], [False])
        thr = best[PEER_TOPK - 1]
        zsum = jnp.zeros_like(thr)
        for r in range(PEER_TOPK):
            zsum = zsum + jnp.exp(best[r] - best[0])
        cnt = jnp.zeros((nk, tp), F32)
        for j in range(PEER_TOPK):
            cj = jnp.sum(((a[j:j + 1, :] + b) >= thr).astype(F32), axis=0, keepdims=True)
            cnt = jnp.where(s1 == a[j:j + 1, :], cj, cnt)
        cnt_ref[hd] = cnt
        c1_ref[hd] = jnp.exp(s1 - a[0:1, :]) / zsum
        packed = (nk // BF16_ROWS, BF16_ROWS, tp)
        rank2_ref[hd] = rank2.reshape(packed).astype(BF16)
        e2_ref[hd] = jnp.exp(s2 - b[0:1, :]).reshape(packed).astype(BF16)
        return 0

    lax.fori_loop(0, PEER_HEADS, head, 0)


def peer_route(x, g, wq, keys, tp):
    m, d = x.shape
    nq = wq.shape[1]
    nk = PEER_KEYS
    kern = functools.partial(_peer_route_kernel, tp=tp)
    tok_blk = lambda i: (0, 0, i)
    packed_blk = lambda i: (0, 0, 0, i)
    packed = (PEER_HEADS, nk // BF16_ROWS, BF16_ROWS)
    return pl.pallas_call(
        kern,
        grid=(m // tp,),
        in_specs=[pl.BlockSpec((tp, d), lambda i: (i, 0)),
                  pl.BlockSpec((1, d), lambda i: (0, 0)),
                  pl.BlockSpec((d, nq), lambda i: (0, 0)),
                  pl.BlockSpec((2 * PEER_HEADS, nk, nk), lambda i: (0, 0, 0))],
        out_specs=[pl.BlockSpec((tp, d), lambda i: (i, 0)),
                   pl.BlockSpec(packed + (tp,), packed_blk),
                   pl.BlockSpec(packed + (tp,), packed_blk),
                   pl.BlockSpec((PEER_HEADS, nk, tp), tok_blk),
                   pl.BlockSpec((PEER_HEADS, nk, tp), tok_blk)],
        out_shape=[jax.ShapeDtypeStruct((m, d), BF16),
                   jax.ShapeDtypeStruct(packed + (m,), BF16),
                   jax.ShapeDtypeStruct(packed + (m,), BF16),
                   jax.ShapeDtypeStruct((PEER_HEADS, nk, m), F32),
                   jax.ShapeDtypeStruct((PEER_HEADS, nk, m), F32)],
        scratch_shapes=[pltpu.VMEM((2 * PEER_HEADS, tp, nk), BF16),
                        pltpu.VMEM((2, PEER_TOPK, tp), F32)],
        compiler_params=_params(("parallel",)),
    )(x, g.reshape(1, d), wq, keys)


def _peer_experts_kernel(x_ref, hb_ref, rank2_ref, e2_ref, cnt_ref, c1_ref, u_ref, vt_ref, fg_ref, o_ref,
                         acc_ref, hw_ref, *, blocks, final_norm):
    e = pl.program_id(1)
    nk = PEER_KEYS
    tt = hb_ref.shape[0]
    tiles = nk // BF16_ROWS

    @pl.when(e == 0)
    def _():
        acc_ref[...] = jnp.zeros_like(acc_ref)

    pre = lax.dot_general(u_ref[...], hb_ref[...], (((1,), (1,)), ((), ())), preferred_element_type=F32)
    for j in range(blocks):
        w = None
        for hd in range(PEER_HEADS):
            cnt = jnp.broadcast_to(cnt_ref[hd, j:j + 1, :], (BF16_ROWS, tt)).astype(BF16)
            c1 = jnp.broadcast_to(c1_ref[hd, j:j + 1, :], (BF16_ROWS, tt)).astype(BF16)
            term = jnp.where(rank2_ref[hd] < cnt[None], e2_ref[hd], jnp.zeros((), BF16)) * c1[None]
            w = term if w is None else w + term
        act = _gelu(pre[j * nk:(j + 1) * nk, :].reshape(tiles, BF16_ROWS, tt).astype(BF16))
        hw_ref[j * tiles:(j + 1) * tiles] = act * w
    hw = hw_ref[...].reshape(blocks * nk, tt)
    acc_ref[...] += jnp.dot(vt_ref[...], hw, preferred_element_type=F32)

    @pl.when(e == pl.num_programs(1) - 1)
    def _():
        out = x_ref[...] + acc_ref[...].T
        o_ref[...] = _rms(out, fg_ref[...]) if final_norm else out


def peer_experts(x, hb, rank2, e2, cnt, c1, u_tabs, vt_tabs, layer, final_g, final_norm, tt, blocks):
    m, d = x.shape
    n_exp = u_tabs.shape[1]
    nk = PEER_KEYS
    te = blocks * nk
    kern = functools.partial(_peer_experts_kernel, blocks=blocks, final_norm=final_norm)
    packed = (PEER_HEADS, nk // BF16_ROWS, BF16_ROWS, tt)
    return pl.pallas_call(
        kern,
        grid=(m // tt, n_exp // te),
        in_specs=[pl.BlockSpec((tt, d), lambda t, e: (t, 0)),
                  pl.BlockSpec((tt, d), lambda t, e: (t, 0)),
                  pl.BlockSpec(packed, lambda t, e: (0, 0, 0, t)),
                  pl.BlockSpec(packed, lambda t, e: (0, 0, 0, t)),
                  pl.BlockSpec((PEER_HEADS, blocks, tt), lambda t, e: (0, e, t)),
                  pl.BlockSpec((PEER_HEADS, blocks, tt), lambda t, e: (0, e, t)),
                  pl.BlockSpec((None, te, d), lambda t, e: (layer, e, 0)),
                  pl.BlockSpec((None, d, te), lambda t, e: (layer, 0, e)),
                  pl.BlockSpec((1, d), lambda t, e: (0, 0))],
        out_specs=pl.BlockSpec((tt, d), lambda t, e: (t, 0)),
        out_shape=jax.ShapeDtypeStruct((m, d), F32),
        scratch_shapes=[pltpu.VMEM((d, tt), F32), pltpu.VMEM((te // BF16_ROWS, BF16_ROWS, tt), BF16)],
        compiler_params=_params(("parallel", "arbitrary")),
    )(x, hb, rank2, e2, cnt, c1, u_tabs, vt_tabs, final_g.reshape(1, d))


def _tiles(seq, n_mem):
    return dict(
        proj=min(512, seq), conv=min(512, seq), sba=min(256, seq), ssm=min(512, seq), mix=min(512, seq),
        memkv=min(256, n_mem), route=min(512, seq), experts=min(1024, seq), blocks=8)


def kernel(x, mem, norm_mix_g, w_in, conv_w, conv_b, conv_ln_g, conv_ln_b, ssm_lambda_re, ssm_lambda_im, ssm_log_dt, ssm_b_re, ssm_b_im, ssm_c_re, ssm_c_im, ssm_d, ssm_glu_w, ssm_glu_b, grp_norm_g, w_out, norm_mem_g, mem_norm_g, w_mq, w_mkv, w_mo, norm_ffn_g, peer_wq, peer_sub_keys, peer_u, peer_v, final_norm_g):
    batch, seq, d = x.shape
    n_mem = mem.shape[1]
    depth = w_in.shape[0]
    d_conv = conv_w.shape[2]
    d_ssm = ssm_d.shape[1]
    d_attn = (w_in.shape[2] - 2 * d_conv - d_ssm) // 3
    t = _tiles(seq, n_mem)

    xf = x.reshape(batch * seq, d)
    memf = mem.reshape(batch * n_mem, d)
    u_tabs = peer_u.astype(BF16)
    vt_tabs = jnp.swapaxes(peer_v, 1, 2).astype(BF16)
    for l in range(depth):
        proj = norm_matmul(xf, norm_mix_g[l], w_in[l].astype(BF16), t['proj'])
        yc = conformer_conv(proj, conv_w[l], conv_b[l], conv_ln_g[l], conv_ln_b[l], seq, t['conv'])
        ya = stick_breaking_attention(proj, batch, seq, 2 * d_conv, d_attn, t['sba'])
        bblk, cblk, pw = _ssm_tables(ssm_lambda_re[l], ssm_lambda_im[l], ssm_log_dt[l], ssm_b_re[l], ssm_b_im[l],
                                     ssm_c_re[l], ssm_c_im[l])
        ys = s5_ssm(proj, batch, seq, 2 * d_conv + 3 * d_attn, bblk, cblk, pw, ssm_d[l], ssm_glu_w[l], ssm_glu_b[l],
                    t['ssm'])
        memkv = norm_matmul(memf, mem_norm_g[l], w_mkv[l].astype(BF16), t['memkv'])
        xf = mix_mem(xf, yc, ya, ys, grp_norm_g[l], w_out[l].astype(BF16), norm_mem_g[l], w_mq[l].astype(BF16), memkv,
                     w_mo[l].astype(BF16), seq, n_mem, t['mix'])

        keys = peer_sub_keys[l].reshape(2 * PEER_HEADS, PEER_KEYS, -1).astype(BF16)
        hb, rank2, e2, cnt, c1 = peer_route(xf, norm_ffn_g[l], peer_wq[l].astype(BF16), keys, t['route'])
        xf = peer_experts(xf, hb, rank2, e2, cnt, c1, u_tabs, vt_tabs, l, final_norm_g, l == depth - 1,
                          t['experts'], t['blocks'])
    return xf.reshape(batch, seq, d)
```

```python
import functools
import math

import jax
import jax.numpy as jnp
from jax import lax
from jax.experimental import pallas as pl
from jax.experimental.pallas import tpu as pltpu

F32 = jnp.float32
BF16 = jnp.bfloat16
EPS = 1e-6

CONV_WIDTH = 31
SUBLANES = 8
CONV_HALO = 32
ATTN_HEAD_DIM = 64
SBA_HEADS_PER_STEP = 4
LOG2E = 1.4426950408889634
UNDERFLOW_LOG2 = -160.0
SSM_GROUP = 16
SSM_STATE = 64
MEM_HEADS = 4
PEER_HEADS = 8
PEER_KEYS = 128
PEER_TOPK = 16
PAIR_CANDIDATES = 80
RANK_NONE = 99.0
BF16_ROWS = 16
VMEM_LIMIT = 50 * 1024 * 1024


def _params(sem, vmem=VMEM_LIMIT):
    return pltpu.CompilerParams(dimension_semantics=sem, vmem_limit_bytes=vmem)


def _rms(x, g):
    ms = jnp.mean(x * x, axis=-1, keepdims=True)
    return x * lax.rsqrt(ms + EPS) * g


def _gelu(x):
    c = math.sqrt(2.0 / math.pi)
    return 0.5 * x * (1.0 + jnp.tanh(c * (x + 0.044715 * (x * x * x))))


def _sigmoid(x):
    return 1.0 / (1.0 + jnp.exp(-x))


def _norm_matmul_kernel(x_ref, g_ref, w_ref, o_ref):
    h = _rms(x_ref[...], g_ref[...]).astype(BF16)
    o_ref[...] = jnp.dot(h, w_ref[...], preferred_element_type=F32)


def norm_matmul(x, g, w, tm):
    m, d = x.shape
    n = w.shape[1]
    return pl.pallas_call(
        _norm_matmul_kernel,
        grid=(m // tm,),
        in_specs=[pl.BlockSpec((tm, d), lambda i: (i, 0)),
                  pl.BlockSpec((1, d), lambda i: (0, 0)),
                  pl.BlockSpec((d, n), lambda i: (0, 0))],
        out_specs=pl.BlockSpec((tm, n), lambda i: (i, 0)),
        out_shape=jax.ShapeDtypeStruct((m, n), F32),
        compiler_params=_params(("parallel",)),
    )(x, g.reshape(1, d), w)


def _conv_kernel(cur_ref, halo_ref, w_ref, b_ref, lng_ref, lnb_ref, o_ref, ext_ref, phase_ref, *, ts, tiles_per_seq,
                 dc):
    i = pl.program_id(0)
    cur = cur_ref[...]
    halo = halo_ref[...]
    y = cur[:, :dc] * _sigmoid(cur[:, dc:])
    yh = halo[:, :dc] * _sigmoid(halo[:, dc:])
    yh = jnp.where(i % tiles_per_seq == 0, 0.0, yh)
    ext_ref[0:CONV_HALO, :] = yh
    ext_ref[CONV_HALO:, :] = y
    span = ts + CONV_HALO - SUBLANES
    for r in range(1, SUBLANES):
        phase_ref[r - 1, 0:span, :] = ext_ref[r:r + span, :]
    acc = jnp.broadcast_to(b_ref[...], (ts, dc))
    for k in range(CONV_WIDTH):
        off = CONV_HALO - (CONV_WIDTH - 1) + k
        phase, base = off % SUBLANES, off - off % SUBLANES
        window = ext_ref[base:base + ts, :] if phase == 0 else phase_ref[phase - 1, base:base + ts, :]
        acc = acc + w_ref[k:k + 1, :] * window
    mu = jnp.mean(acc, axis=-1, keepdims=True)
    cen = acc - mu
    var = jnp.mean(cen * cen, axis=-1, keepdims=True)
    z = cen * lax.rsqrt(var + EPS) * lng_ref[...] + lnb_ref[...]
    o_ref[...] = z * _sigmoid(z)


def conformer_conv(proj, conv_w, conv_b, ln_g, ln_b, seq, ts):
    m = proj.shape[0]
    dc = conv_w.shape[1]
    kern = functools.partial(_conv_kernel, ts=ts, tiles_per_seq=seq // ts, dc=dc)
    halo_blocks = ts // CONV_HALO
    return pl.pallas_call(
        kern,
        grid=(m // ts,),
        in_specs=[pl.BlockSpec((ts, 2 * dc), lambda i: (i, 0)),
                  pl.BlockSpec((CONV_HALO, 2 * dc), lambda i: (jnp.maximum(i * halo_blocks - 1, 0), 0)),
                  pl.BlockSpec((CONV_WIDTH, dc), lambda i: (0, 0)),
                  pl.BlockSpec((1, dc), lambda i: (0, 0)),
                  pl.BlockSpec((1, dc), lambda i: (0, 0)),
                  pl.BlockSpec((1, dc), lambda i: (0, 0))],
        out_specs=pl.BlockSpec((ts, dc), lambda i: (i, 0)),
        out_shape=jax.ShapeDtypeStruct((m, dc), F32),
        scratch_shapes=[pltpu.VMEM((ts + CONV_HALO, dc), F32),
                        pltpu.VMEM((SUBLANES - 1, ts + CONV_HALO - SUBLANES, dc), F32)],
        compiler_params=_params(("parallel",)),
    )(proj, proj, conv_w, conv_b.reshape(1, dc), ln_g.reshape(1, dc), ln_b.reshape(1, dc))


def _sba_kernel(q_ref, k_ref, v_ref, o_ref, *, tq, scale):
    i = pl.program_id(2)
    dh = ATTN_HEAD_DIM
    row = lax.broadcasted_iota(jnp.int32, (tq, tq), 0)
    col = lax.broadcasted_iota(jnp.int32, (tq, tq), 1)
    causal = col < row
    suffix = (row > col).astype(BF16)

    heads = q_ref.shape[1] // dh
    lanes = [slice(hh * dh, (hh + 1) * dh) for hh in range(heads)]
    qs = [(q_ref[:, ln] * (scale * LOG2E)).astype(BF16) for ln in lanes]

    def block(j, carry, diagonal):
        start = pl.multiple_of(j * tq, tq)
        hs = range(heads)
        kj = [k_ref[pl.ds(start, tq), lanes[h]].astype(BF16) for h in hs]
        vj = [v_ref[pl.ds(start, tq), lanes[h]].astype(BF16) for h in hs]
        z = [lax.dot_general(qs[h], kj[h], (((1,), (1,)), ((), ())), preferred_element_type=F32) for h in hs]
        log_beta, log_stay, later = [], [], []
        for h in hs:
            sp = jnp.log2(1.0 + jnp.exp2(-jnp.abs(z[h])))
            log_beta.append(jnp.minimum(z[h], 0.0) - sp)
            stay = -jnp.maximum(z[h], 0.0) - sp
            log_stay.append(jnp.where(causal, stay, 0.0) if diagonal else stay)
        for h in hs:
            later.append(jnp.dot(log_stay[h].astype(BF16), suffix, preferred_element_type=F32))
        out = []
        for h in hs:
            acc, run = carry[h]
            a = jnp.exp2(log_beta[h] + later[h] + run)
            if diagonal:
                a = jnp.where(causal, a, 0.0)
            acc = acc + jnp.dot(a.astype(BF16), vj[h], preferred_element_type=F32)
            run = run + later[h][:, 0:1] + log_stay[h][:, 0:1]
            out.append((acc, run))
        return tuple(out)

    def unfinished(c):
        return jnp.logical_and(c[0] < i, c[1])

    def sweep(c):
        jj, _, carry = c
        carry = block(i - 1 - jj, carry, False)
        top = carry[0][1]
        for h in range(1, heads):
            top = jnp.maximum(top, carry[h][1])
        return jj + 1, jnp.max(top) > UNDERFLOW_LOG2, carry

    carry = tuple((jnp.zeros((tq, dh), F32), jnp.zeros((tq, 1), F32)) for _ in range(heads))
    carry = block(i, carry, True)
    _, _, carry = lax.while_loop(unfinished, sweep, (jnp.int32(0), jnp.bool_(True), carry))
    for hh in range(heads):
        o_ref[:, lanes[hh]] = carry[hh][0]


def stick_breaking_attention(proj, batch, seq, col0, d_attn, tq):
    lane_blk = SBA_HEADS_PER_STEP * ATTN_HEAD_DIM
    hp = d_attn // lane_blk
    qb, kb, vb = col0 // lane_blk, (col0 + d_attn) // lane_blk, (col0 + 2 * d_attn) // lane_blk
    nq = seq // tq
    kern = functools.partial(_sba_kernel, tq=tq, scale=1.0 / math.sqrt(ATTN_HEAD_DIM))
    return pl.pallas_call(
        kern,
        grid=(batch, hp, nq),
        in_specs=[pl.BlockSpec((tq, lane_blk), lambda b, h, i: (b * nq + i, qb + h)),
                  pl.BlockSpec((seq, lane_blk), lambda b, h, i: (b, kb + h)),
                  pl.BlockSpec((seq, lane_blk), lambda b, h, i: (b, vb + h))],
        out_specs=pl.BlockSpec((tq, lane_blk), lambda b, h, i: (b * nq + i, h)),
        out_shape=jax.ShapeDtypeStruct((batch * seq, d_attn), F32),
        compiler_params=_params(("parallel", "parallel", "arbitrary")),
    )(proj, proj, proj)


def _ssm_kernel(u_ref, bblk_ref, pw_ref, cblk_ref, d_ref, gw_ref, gb_ref, o_ref, st_ref, x_ref, *, ts, n):
    c = pl.program_id(1)

    @pl.when(c == 0)
    def _():
        st_ref[...] = jnp.zeros_like(st_ref)

    u = u_ref[...]
    x_ref[...] = jnp.dot(u.astype(BF16), bblk_ref[...], preferred_element_type=F32)

    def group(r, carry):
        sr, si = carry
        rows = pl.ds(pl.multiple_of(r * 8, 8), 8)
        re = x_ref[rows, 0:n]
        im = x_ref[rows, n:2 * n]
        for lvl, d in enumerate((1, 2, 4)):
            lr = pw_ref[lvl, :, 0:n]
            li = pw_ref[lvl, :, n:2 * n]
            pr = pltpu.roll(re, d, 0)
            pi = pltpu.roll(im, d, 0)
            re, im = re + pr * lr - pi * li, im + pr * li + pi * lr
        lr = pw_ref[3, :, 0:n]
        li = pw_ref[3, :, n:2 * n]
        re, im = re + sr * lr - si * li, im + sr * li + si * lr
        x_ref[rows, 0:n] = re
        x_ref[rows, n:2 * n] = im
        return (jnp.broadcast_to(re[7:8, :], (8, n)), jnp.broadcast_to(im[7:8, :], (8, n)))

    sr, si = lax.fori_loop(0, ts // 8, group, (st_ref[:, 0:n], st_ref[:, n:2 * n]))
    st_ref[:, 0:n] = sr
    st_ref[:, n:2 * n] = si

    y = jnp.dot(x_ref[...].astype(BF16), cblk_ref[...], preferred_element_type=F32)
    y = _gelu(y + d_ref[...] * u)
    gate = jnp.dot(y.astype(BF16), gw_ref[...], preferred_element_type=F32) + gb_ref[...]
    o_ref[...] = y * _sigmoid(gate)


def _ssm_tables(lam_re, lam_im, log_dt, b_re, b_im, c_re, c_im):
    g, p = lam_re.shape
    hch = b_re.shape[2]
    n = g * p
    lr = jnp.minimum(lam_re.astype(F32), -1e-4)
    li = lam_im.astype(F32)
    dt = jnp.exp(log_dt.astype(F32))[:, None]

    def lam_pow(steps):
        mag = jnp.exp(lr * dt * steps)
        return mag * jnp.cos(li * dt * steps), mag * jnp.sin(li * dt * steps)

    br, bi = lam_pow(1.0)
    norm = lr * lr + li * li
    cr = ((br - 1.0) * lr + bi * li) / norm
    ci = (bi * lr - (br - 1.0) * li) / norm
    bre, bim = b_re.astype(F32), b_im.astype(F32)
    bbar_re = cr[..., None] * bre - ci[..., None] * bim
    bbar_im = cr[..., None] * bim + ci[..., None] * bre
    eye = jnp.eye(g, dtype=F32)
    bb_re = jnp.einsum('gph,gk->ghkp', bbar_re, eye).reshape(g * hch, n)
    bb_im = jnp.einsum('gph,gk->ghkp', bbar_im, eye).reshape(g * hch, n)
    bblk = jnp.concatenate([bb_re, bb_im], axis=1).astype(BF16)
    cc_re = jnp.einsum('ghp,gk->kpgh', c_re.astype(F32), eye).reshape(n, g * hch)
    cc_im = jnp.einsum('ghp,gk->kpgh', c_im.astype(F32), eye).reshape(n, g * hch)
    cblk = jnp.concatenate([cc_re, -cc_im], axis=0).astype(BF16)
    t = jnp.arange(8)
    tabs = []
    for d in (1, 2, 4):
        pr, pi = lam_pow(float(d))
        keep = (t >= d)[:, None]
        tabs.append(jnp.concatenate([jnp.where(keep, pr.reshape(1, n), 0.0),
                                     jnp.where(keep, pi.reshape(1, n), 0.0)], axis=1))
    rows = [lam_pow(float(k + 1)) for k in range(8)]
    tabs.append(jnp.concatenate([jnp.stack([r[0].reshape(n) for r in rows]),
                                 jnp.stack([r[1].reshape(n) for r in rows])], axis=1))
    return bblk, cblk, jnp.stack(tabs).astype(F32)


def s5_ssm(proj, batch, seq, col0, bblk, cblk, pw, d_skip, glu_w, glu_b, ts):
    ds = bblk.shape[0]
    n = bblk.shape[1] // 2
    nt = seq // ts
    kern = functools.partial(_ssm_kernel, ts=ts, n=n)
    cb = col0 // ds
    return pl.pallas_call(
        kern,
        grid=(batch, nt),
        in_specs=[pl.BlockSpec((ts, ds), lambda b, c: (b * nt + c, cb)),
                  pl.BlockSpec((ds, 2 * n), lambda b, c: (0, 0)),
                  pl.BlockSpec((4, 8, 2 * n), lambda b, c: (0, 0, 0)),
                  pl.BlockSpec((2 * n, ds), lambda b, c: (0, 0)),
                  pl.BlockSpec((1, ds), lambda b, c: (0, 0)),
                  pl.BlockSpec((ds, ds), lambda b, c: (0, 0)),
                  pl.BlockSpec((1, ds), lambda b, c: (0, 0))],
        out_specs=pl.BlockSpec((ts, ds), lambda b, c: (b * nt + c, 0)),
        out_shape=jax.ShapeDtypeStruct((batch * seq, ds), F32),
        scratch_shapes=[pltpu.VMEM((8, 2 * n), F32), pltpu.VMEM((ts, 2 * n), F32)],
        compiler_params=_params(("parallel", "arbitrary")),
    )(proj, bblk, pw, cblk, d_skip.reshape(1, ds), glu_w.astype(BF16), glu_b.reshape(1, ds))


def _mix_mem_kernel(x_ref, yc_ref, ya_ref, ys_ref, gg_ref, wout_ref, g_ref, wq_ref, kv_ref, wo_ref, o_ref, *, heads):
    x = x_ref[...]
    off = 0
    for y_ref in (yc_ref, ya_ref, ys_ref):
        width = y_ref.shape[1]
        h = _rms(y_ref[...], gg_ref[:, off:off + width]).astype(BF16)
        x = x + jnp.dot(h, wout_ref[off:off + width, :], preferred_element_type=F32)
        off += width
    d = x.shape[1]
    dh = d // heads
    h = _rms(x, g_ref[...]).astype(BF16)
    q = jnp.dot(h, wq_ref[...], preferred_element_type=F32) * (1.0 / math.sqrt(dh))
    outs = []
    for hd in range(heads):
        qh = q[:, hd * dh:(hd + 1) * dh].astype(BF16)
        kh = kv_ref[:, hd * dh:(hd + 1) * dh].astype(BF16)
        vh = kv_ref[:, d + hd * dh:d + (hd + 1) * dh].astype(BF16)
        s = lax.dot_general(qh, kh, (((1,), (1,)), ((), ())), preferred_element_type=F32)
        e = jnp.exp(s - jnp.max(s, axis=-1, keepdims=True))
        p = e / jnp.sum(e, axis=-1, keepdims=True)
        outs.append(jnp.dot(p.astype(BF16), vh, preferred_element_type=F32).astype(BF16))
    o = jnp.concatenate(outs, axis=-1)
    o_ref[...] = x + jnp.dot(o, wo_ref[...], preferred_element_type=F32)


def mix_mem(x, yc, ya, ys, g_grp, w_out, g_mem, wq, memkv, wo, seq, n_mem, tm):
    m, d = x.shape
    dc, da, dsm = yc.shape[1], ya.shape[1], ys.shape[1]
    tiles_per_seq = seq // tm
    kern = functools.partial(_mix_mem_kernel, heads=MEM_HEADS)
    row_blk = lambda i: (i, 0)
    whole = lambda i: (0, 0)
    return pl.pallas_call(
        kern,
        grid=(m // tm,),
        in_specs=[pl.BlockSpec((tm, d), row_blk),
                  pl.BlockSpec((tm, dc), row_blk),
                  pl.BlockSpec((tm, da), row_blk),
                  pl.BlockSpec((tm, dsm), row_blk),
                  pl.BlockSpec((1, dc + da + dsm), whole),
                  pl.BlockSpec((dc + da + dsm, d), whole),
                  pl.BlockSpec((1, d), whole),
                  pl.BlockSpec((d, d), whole),
                  pl.BlockSpec((n_mem, 2 * d), lambda i: (i // tiles_per_seq, 0)),
                  pl.BlockSpec((d, d), whole)],
        out_specs=pl.BlockSpec((tm, d), row_blk),
        out_shape=jax.ShapeDtypeStruct((m, d), F32),
        compiler_params=_params(("parallel",)),
    )(x, yc, ya, ys, g_grp.reshape(1, -1), w_out, g_mem.reshape(1, d), wq, memkv, wo)


def _sort16_network():
    def merge(lo, hi, r):
        step = r * 2
        if step < hi - lo:
            yield from merge(lo, hi, step)
            yield from merge(lo + r, hi, step)
            yield from [(i, i + r) for i in range(lo + r, hi - r, step)]
        else:
            yield (lo, lo + r)

    def sort(lo, hi):
        if hi - lo >= 1:
            mid = lo + (hi - lo) // 2
            yield from sort(lo, mid)
            yield from sort(mid + 1, hi)
            yield from merge(lo, hi, 1)

    return tuple(sort(0, PEER_TOPK - 1))


_SORT16 = _sort16_network()


def _exchange(v, i, j):
    v[i], v[j] = jnp.maximum(v[i], v[j]), jnp.minimum(v[i], v[j])


def _top16_sorted(x):
    v = [x[i * SUBLANES:(i + 1) * SUBLANES, :] for i in range(PEER_TOPK)]
    for i, j in _SORT16:
        _exchange(v, i, j)
    for shift in (4, 2, 1):
        v = [jnp.maximum(v[i], pltpu.roll(v[PEER_TOPK - 1 - i], shift, 0)) for i in range(PEER_TOPK)]
        for d in (8, 4, 2, 1):
            for i in range(PEER_TOPK):
                if not i & d:
                    _exchange(v, i, i + d)
    return v


def _peer_route_kernel(x_ref, g_ref, wq_ref, keys_ref, hb_ref, rank2_ref, e2_ref, cnt_ref, c1_ref,
                       q_scr, top_scr, *, tp):
    nk = PEER_KEYS
    h = _rms(x_ref[...], g_ref[...]).astype(BF16)
    hb_ref[...] = h
    q = jnp.dot(h, wq_ref[...], preferred_element_type=F32)
    for c in range(2 * PEER_HEADS):
        q_scr[c] = q[:, c * nk:(c + 1) * nk].astype(BF16)

    def head(hd, _):
        nt = (((1,), (1,)), ((), ()))
        s1 = lax.dot_general(keys_ref[2 * hd], q_scr[2 * hd], nt, preferred_element_type=F32)
        s2 = lax.dot_general(keys_ref[2 * hd + 1], q_scr[2 * hd + 1], nt, preferred_element_type=F32)
        va = _top16_sorted(s1)
        vb = _top16_sorted(s2)
        for r in range(PEER_TOPK):
            top_scr[0, r:r + 1, :] = va[r][0:1, :]
            top_scr[1, r:r + 1, :] = vb[r][0:1, :]
        a = top_scr[0]
        b = top_scr[1]
        cands = [a[0:1, :] + b]
        for j in range(1, 8):
            cands.append(a[j:j + 1, :] + b[0:8, :])
        cands.append(a[8:16, :] + b[0:1, :])
        cands.append(jnp.full((nk - PAIR_CANDIDATES, tp), -jnp.inf, F32))
        best = _top16_sorted(jnp.concatenate(cands, axis=0))
        thr = best[PEER_TOPK - 1][0:1, :]
        zsum = jnp.zeros_like(best[0])
        for r in range(PEER_TOPK):
            zsum = zsum + jnp.exp(best[r] - best[0])
        groups = (PEER_TOPK, SUBLANES, tp)
        s1g = s1.reshape(groups)
        s2g = s2.reshape(groups)
        cnt = jnp.zeros(groups, F32)
        for j in range(PEER_TOPK):
            cj = jnp.sum(((a[j:j + 1, :] + b) >= thr).astype(F32), axis=0, keepdims=True)
            cnt = jnp.where(s1g == va[j][None], cj[None], cnt)
        rank2 = jnp.full(groups, RANK_NONE, F32)
        for r in reversed(range(PEER_TOPK)):
            rank2 = jnp.where(s2g == vb[r][None], float(r), rank2)
        cnt_ref[hd] = cnt.reshape(nk, tp)
        c1_ref[hd] = (jnp.exp(s1g - va[0][None]) / zsum[None]).reshape(nk, tp)
        packed = (nk // BF16_ROWS, BF16_ROWS, tp)
        rank2_ref[hd] = rank2.reshape(packed).astype(BF16)
        e2_ref[hd] = jnp.exp(s2g - vb[0][None]).reshape(packed).astype(BF16)
        return 0

    lax.fori_loop(0, PEER_HEADS, head, 0)


def peer_route(x, g, wq, keys, tp):
    m, d = x.shape
    nq = wq.shape[1]
    nk = PEER_KEYS
    kern = functools.partial(_peer_route_kernel, tp=tp)
    tok_blk = lambda i: (0, 0, i)
    packed_blk = lambda i: (0, 0, 0, i)
    packed = (PEER_HEADS, nk // BF16_ROWS, BF16_ROWS)
    return pl.pallas_call(
        kern,
        grid=(m // tp,),
        in_specs=[pl.BlockSpec((tp, d), lambda i: (i, 0)),
                  pl.BlockSpec((1, d), lambda i: (0, 0)),
                  pl.BlockSpec((d, nq), lambda i: (0, 0)),
                  pl.BlockSpec((2 * PEER_HEADS, nk, nk), lambda i: (0, 0, 0))],
        out_specs=[pl.BlockSpec((tp, d), lambda i: (i, 0)),
                   pl.BlockSpec(packed + (tp,), packed_blk),
                   pl.BlockSpec(packed + (tp,), packed_blk),
                   pl.BlockSpec((PEER_HEADS, nk, tp), tok_blk),
                   pl.BlockSpec((PEER_HEADS, nk, tp), tok_blk)],
        out_shape=[jax.ShapeDtypeStruct((m, d), BF16),
                   jax.ShapeDtypeStruct(packed + (m,), BF16),
                   jax.ShapeDtypeStruct(packed + (m,), BF16),
                   jax.ShapeDtypeStruct((PEER_HEADS, nk, m), F32),
                   jax.ShapeDtypeStruct((PEER_HEADS, nk, m), F32)],
        scratch_shapes=[pltpu.VMEM((2 * PEER_HEADS, tp, nk), BF16),
                        pltpu.VMEM((2, PEER_TOPK, tp), F32)],
        compiler_params=_params(("parallel",)),
    )(x, g.reshape(1, d), wq, keys)


def _peer_experts_kernel(x_ref, hb_ref, rank2_ref, e2_ref, cnt_ref, c1_ref, u_ref, vt_ref, fg_ref, o_ref,
                         acc_ref, hw_ref, *, blocks, final_norm):
    e = pl.program_id(1)
    nk = PEER_KEYS
    tt = hb_ref.shape[0]
    tiles = nk // BF16_ROWS

    @pl.when(e == 0)
    def _():
        acc_ref[...] = jnp.zeros_like(acc_ref)

    pre = lax.dot_general(u_ref[...], hb_ref[...], (((1,), (1,)), ((), ())), preferred_element_type=F32)
    for j in range(blocks):
        w = None
        for hd in range(PEER_HEADS):
            cnt = jnp.broadcast_to(cnt_ref[hd, j:j + 1, :], (BF16_ROWS, tt)).astype(BF16)
            c1 = jnp.broadcast_to(c1_ref[hd, j:j + 1, :], (BF16_ROWS, tt)).astype(BF16)
            term = jnp.where(rank2_ref[hd] < cnt[None], e2_ref[hd], jnp.zeros((), BF16)) * c1[None]
            w = term if w is None else w + term
        act = _gelu(pre[j * nk:(j + 1) * nk, :].reshape(tiles, BF16_ROWS, tt).astype(BF16))
        hw_ref[j * tiles:(j + 1) * tiles] = act * w
    hw = hw_ref[...].reshape(blocks * nk, tt)
    acc_ref[...] += jnp.dot(vt_ref[...], hw, preferred_element_type=F32)

    @pl.when(e == pl.num_programs(1) - 1)
    def _():
        out = x_ref[...] + acc_ref[...].T
        o_ref[...] = _rms(out, fg_ref[...]) if final_norm else out


def peer_experts(x, hb, rank2, e2, cnt, c1, u_tabs, vt_tabs, layer, final_g, final_norm, tt, blocks):
    m, d = x.shape
    n_exp = u_tabs.shape[1]
    nk = PEER_KEYS
    te = blocks * nk
    kern = functools.partial(_peer_experts_kernel, blocks=blocks, final_norm=final_norm)
    packed = (PEER_HEADS, nk // BF16_ROWS, BF16_ROWS, tt)
    return pl.pallas_call(
        kern,
        grid=(m // tt, n_exp // te),
        in_specs=[pl.BlockSpec((tt, d), lambda t, e: (t, 0)),
                  pl.BlockSpec((tt, d), lambda t, e: (t, 0)),
                  pl.BlockSpec(packed, lambda t, e: (0, 0, 0, t)),
                  pl.BlockSpec(packed, lambda t, e: (0, 0, 0, t)),
                  pl.BlockSpec((PEER_HEADS, blocks, tt), lambda t, e: (0, e, t)),
                  pl.BlockSpec((PEER_HEADS, blocks, tt), lambda t, e: (0, e, t)),
                  pl.BlockSpec((None, te, d), lambda t, e: (layer, e, 0)),
                  pl.BlockSpec((None, d, te), lambda t, e: (layer, 0, e)),
                  pl.BlockSpec((1, d), lambda t, e: (0, 0))],
        out_specs=pl.BlockSpec((tt, d), lambda t, e: (t, 0)),
        out_shape=jax.ShapeDtypeStruct((m, d), F32),
        scratch_shapes=[pltpu.VMEM((d, tt), F32), pltpu.VMEM((te // BF16_ROWS, BF16_ROWS, tt), BF16)],
        compiler_params=_params(("parallel", "arbitrary")),
    )(x, hb, rank2, e2, cnt, c1, u_tabs, vt_tabs, final_g.reshape(1, d))


def _tiles(seq, n_mem):
    return dict(
        proj=min(512, seq), conv=min(512, seq), sba=min(256, seq), ssm=min(512, seq), mix=min(512, seq),
        memkv=min(256, n_mem), route=min(512, seq), experts=min(1024, seq), blocks=8)


def kernel(x, mem, norm_mix_g, w_in, conv_w, conv_b, conv_ln_g, conv_ln_b, ssm_lambda_re, ssm_lambda_im, ssm_log_dt, ssm_b_re, ssm_b_im, ssm_c_re, ssm_c_im, ssm_d, ssm_glu_w, ssm_glu_b, grp_norm_g, w_out, norm_mem_g, mem_norm_g, w_mq, w_mkv, w_mo, norm_ffn_g, peer_wq, peer_sub_keys, peer_u, peer_v, final_norm_g):
    batch, seq, d = x.shape
    n_mem = mem.shape[1]
    depth = w_in.shape[0]
    d_conv = conv_w.shape[2]
    d_ssm = ssm_d.shape[1]
    d_attn = (w_in.shape[2] - 2 * d_conv - d_ssm) // 3
    t = _tiles(seq, n_mem)

    xf = x.reshape(batch * seq, d)
    memf = mem.reshape(batch * n_mem, d)
    u_tabs = peer_u.astype(BF16)
    vt_tabs = jnp.swapaxes(peer_v, 1, 2).astype(BF16)
    for l in range(depth):
        proj = norm_matmul(xf, norm_mix_g[l], w_in[l].astype(BF16), t['proj'])
        yc = conformer_conv(proj, conv_w[l], conv_b[l], conv_ln_g[l], conv_ln_b[l], seq, t['conv'])
        ya = stick_breaking_attention(proj, batch, seq, 2 * d_conv, d_attn, t['sba'])
        bblk, cblk, pw = _ssm_tables(ssm_lambda_re[l], ssm_lambda_im[l], ssm_log_dt[l], ssm_b_re[l], ssm_b_im[l],
                                     ssm_c_re[l], ssm_c_im[l])
        ys = s5_ssm(proj, batch, seq, 2 * d_conv + 3 * d_attn, bblk, cblk, pw, ssm_d[l], ssm_glu_w[l], ssm_glu_b[l],
                    t['ssm'])
        memkv = norm_matmul(memf, mem_norm_g[l], w_mkv[l].astype(BF16), t['memkv'])
        xf = mix_mem(xf, yc, ya, ys, grp_norm_g[l], w_out[l].astype(BF16), norm_mem_g[l], w_mq[l].astype(BF16), memkv,
                     w_mo[l].astype(BF16), seq, n_mem, t['mix'])

        keys = peer_sub_keys[l].reshape(2 * PEER_HEADS, PEER_KEYS, -1).astype(BF16)
        hb, rank2, e2, cnt, c1 = peer_route(xf, norm_ffn_g[l], peer_wq[l].astype(BF16), keys, t['route'])
        xf = peer_experts(xf, hb, rank2, e2, cnt, c1, u_tabs, vt_tabs, l, final_norm_g, l == depth - 1,
                          t['experts'], t['blocks'])
    return xf.reshape(batch, seq, d)
```

```python
import functools
import math

import jax
import jax.numpy as jnp
from jax import lax
from jax.experimental import pallas as pl
from jax.experimental.pallas import tpu as pltpu

F32 = jnp.float32
BF16 = jnp.bfloat16
EPS = 1e-6

CONV_WIDTH = 31
SUBLANES = 8
CONV_HALO = 32
ATTN_HEAD_DIM = 64
SBA_HEADS_PER_STEP = 4
LOG2E = 1.4426950408889634
UNDERFLOW_LOG2 = -160.0
SSM_GROUP = 16
SSM_STATE = 64
MEM_HEADS = 4
PEER_HEADS = 8
PEER_KEYS = 128
PEER_TOPK = 16
PAIR_CANDIDATES = 80
RANK_NONE = 99.0
BF16_ROWS = 16
VMEM_LIMIT = 50 * 1024 * 1024


def _params(sem, vmem=VMEM_LIMIT):
    return pltpu.CompilerParams(dimension_semantics=sem, vmem_limit_bytes=vmem)


def _rms(x, g):
    ms = jnp.mean(x * x, axis=-1, keepdims=True)
    return x * lax.rsqrt(ms + EPS) * g


def _gelu(x):
    c = math.sqrt(2.0 / math.pi)
    return 0.5 * x * (1.0 + jnp.tanh(c * (x + 0.044715 * (x * x * x))))


def _sigmoid(x):
    return 1.0 / (1.0 + jnp.exp(-x))


def _norm_matmul_kernel(x_ref, g_ref, w_ref, o_ref):
    h = _rms(x_ref[...], g_ref[...]).astype(BF16)
    o_ref[...] = jnp.dot(h, w_ref[...], preferred_element_type=F32)


def norm_matmul(x, g, w, tm):
    m, d = x.shape
    n = w.shape[1]
    return pl.pallas_call(
        _norm_matmul_kernel,
        grid=(m // tm,),
        in_specs=[pl.BlockSpec((tm, d), lambda i: (i, 0)),
                  pl.BlockSpec((1, d), lambda i: (0, 0)),
                  pl.BlockSpec((d, n), lambda i: (0, 0))],
        out_specs=pl.BlockSpec((tm, n), lambda i: (i, 0)),
        out_shape=jax.ShapeDtypeStruct((m, n), F32),
        compiler_params=_params(("parallel",)),
    )(x, g.reshape(1, d), w)


def _conv_kernel(cur_ref, halo_ref, w_ref, b_ref, lng_ref, lnb_ref, o_ref, ext_ref, phase_ref, *, ts, tiles_per_seq,
                 dc):
    i = pl.program_id(0)
    cur = cur_ref[...]
    halo = halo_ref[...]
    y = cur[:, :dc] * _sigmoid(cur[:, dc:])
    yh = halo[:, :dc] * _sigmoid(halo[:, dc:])
    yh = jnp.where(i % tiles_per_seq == 0, 0.0, yh)
    ext_ref[0:CONV_HALO, :] = yh
    ext_ref[CONV_HALO:, :] = y
    span = ts + CONV_HALO - SUBLANES
    for r in range(1, SUBLANES):
        phase_ref[r - 1, 0:span, :] = ext_ref[r:r + span, :]
    acc = jnp.broadcast_to(b_ref[...], (ts, dc))
    for k in range(CONV_WIDTH):
        off = CONV_HALO - (CONV_WIDTH - 1) + k
        phase, base = off % SUBLANES, off - off % SUBLANES
        window = ext_ref[base:base + ts, :] if phase == 0 else phase_ref[phase - 1, base:base + ts, :]
        acc = acc + w_ref[k:k + 1, :] * window
    mu = jnp.mean(acc, axis=-1, keepdims=True)
    cen = acc - mu
    var = jnp.mean(cen * cen, axis=-1, keepdims=True)
    z = cen * lax.rsqrt(var + EPS) * lng_ref[...] + lnb_ref[...]
    o_ref[...] = z * _sigmoid(z)


def conformer_conv(proj, conv_w, conv_b, ln_g, ln_b, seq, ts):
    m = proj.shape[0]
    dc = conv_w.shape[1]
    kern = functools.partial(_conv_kernel, ts=ts, tiles_per_seq=seq // ts, dc=dc)
    halo_blocks = ts // CONV_HALO
    return pl.pallas_call(
        kern,
        grid=(m // ts,),
        in_specs=[pl.BlockSpec((ts, 2 * dc), lambda i: (i, 0)),
                  pl.BlockSpec((CONV_HALO, 2 * dc), lambda i: (jnp.maximum(i * halo_blocks - 1, 0), 0)),
                  pl.BlockSpec((CONV_WIDTH, dc), lambda i: (0, 0)),
                  pl.BlockSpec((1, dc), lambda i: (0, 0)),
                  pl.BlockSpec((1, dc), lambda i: (0, 0)),
                  pl.BlockSpec((1, dc), lambda i: (0, 0))],
        out_specs=pl.BlockSpec((ts, dc), lambda i: (i, 0)),
        out_shape=jax.ShapeDtypeStruct((m, dc), F32),
        scratch_shapes=[pltpu.VMEM((ts + CONV_HALO, dc), F32),
                        pltpu.VMEM((SUBLANES - 1, ts + CONV_HALO - SUBLANES, dc), F32)],
        compiler_params=_params(("parallel",)),
    )(proj, proj, conv_w, conv_b.reshape(1, dc), ln_g.reshape(1, dc), ln_b.reshape(1, dc))


def _sba_kernel(q_ref, k_ref, v_ref, o_ref, *, tq, scale):
    i = pl.program_id(2)
    dh = ATTN_HEAD_DIM
    row = lax.broadcasted_iota(jnp.int32, (tq, tq), 0)
    col = lax.broadcasted_iota(jnp.int32, (tq, tq), 1)
    causal = col < row
    suffix = (row > col).astype(BF16)

    heads = q_ref.shape[1] // dh
    lanes = [slice(hh * dh, (hh + 1) * dh) for hh in range(heads)]
    qs = [(q_ref[:, ln] * (scale * LOG2E)).astype(BF16) for ln in lanes]

    def block(j, carry, diagonal):
        start = pl.multiple_of(j * tq, tq)
        hs = range(heads)
        kj = [k_ref[pl.ds(start, tq), lanes[h]].astype(BF16) for h in hs]
        vj = [v_ref[pl.ds(start, tq), lanes[h]].astype(BF16) for h in hs]
        z = [lax.dot_general(qs[h], kj[h], (((1,), (1,)), ((), ())), preferred_element_type=F32) for h in hs]
        log_beta, log_stay, later = [], [], []
        for h in hs:
            sp = jnp.log2(1.0 + jnp.exp2(-jnp.abs(z[h])))
            log_beta.append(jnp.minimum(z[h], 0.0) - sp)
            stay = -jnp.maximum(z[h], 0.0) - sp
            log_stay.append(jnp.where(causal, stay, 0.0) if diagonal else stay)
        for h in hs:
            later.append(jnp.dot(log_stay[h].astype(BF16), suffix, preferred_element_type=F32))
        out = []
        for h in hs:
            acc, run = carry[h]
            a = jnp.exp2(log_beta[h] + later[h] + run)
            if diagonal:
                a = jnp.where(causal, a, 0.0)
            acc = acc + jnp.dot(a.astype(BF16), vj[h], preferred_element_type=F32)
            run = run + later[h][:, 0:1] + log_stay[h][:, 0:1]
            out.append((acc, run))
        return tuple(out)

    def unfinished(c):
        return jnp.logical_and(c[0] < i, c[1])

    def sweep(c):
        jj, _, carry = c
        carry = block(i - 1 - jj, carry, False)
        top = carry[0][1]
        for h in range(1, heads):
            top = jnp.maximum(top, carry[h][1])
        return jj + 1, jnp.max(top) > UNDERFLOW_LOG2, carry

    carry = tuple((jnp.zeros((tq, dh), F32), jnp.zeros((tq, 1), F32)) for _ in range(heads))
    carry = block(i, carry, True)
    _, _, carry = lax.while_loop(unfinished, sweep, (jnp.int32(0), jnp.bool_(True), carry))
    for hh in range(heads):
        o_ref[:, lanes[hh]] = carry[hh][0]


def stick_breaking_attention(proj, batch, seq, col0, d_attn, tq):
    lane_blk = SBA_HEADS_PER_STEP * ATTN_HEAD_DIM
    hp = d_attn // lane_blk
    qb, kb, vb = col0 // lane_blk, (col0 + d_attn) // lane_blk, (col0 + 2 * d_attn) // lane_blk
    nq = seq // tq
    kern = functools.partial(_sba_kernel, tq=tq, scale=1.0 / math.sqrt(ATTN_HEAD_DIM))
    return pl.pallas_call(
        kern,
        grid=(batch, hp, nq),
        in_specs=[pl.BlockSpec((tq, lane_blk), lambda b, h, i: (b * nq + i, qb + h)),
                  pl.BlockSpec((seq, lane_blk), lambda b, h, i: (b, kb + h)),
                  pl.BlockSpec((seq, lane_blk), lambda b, h, i: (b, vb + h))],
        out_specs=pl.BlockSpec((tq, lane_blk), lambda b, h, i: (b * nq + i, h)),
        out_shape=jax.ShapeDtypeStruct((batch * seq, d_attn), F32),
        compiler_params=_params(("parallel", "parallel", "arbitrary")),
    )(proj, proj, proj)


def _ssm_kernel(u_ref, bblk_ref, pw_ref, cblk_ref, d_ref, gw_ref, gb_ref, o_ref, st_ref, x_ref, *, ts, n):
    c = pl.program_id(1)

    @pl.when(c == 0)
    def _():
        st_ref[...] = jnp.zeros_like(st_ref)

    u = u_ref[...]
    x_ref[...] = jnp.dot(u.astype(BF16), bblk_ref[...], preferred_element_type=F32)

    def group(r, carry):
        sr, si = carry
        rows = pl.ds(r * 8, 8)
        re = x_ref[rows, 0:n]
        im = x_ref[rows, n:2 * n]
        for lvl, d in enumerate((1, 2, 4)):
            lr = pw_ref[lvl, :, 0:n]
            li = pw_ref[lvl, :, n:2 * n]
            pr = pltpu.roll(re, d, 0)
            pi = pltpu.roll(im, d, 0)
            re, im = re + pr * lr - pi * li, im + pr * li + pi * lr
        lr = pw_ref[3, :, 0:n]
        li = pw_ref[3, :, n:2 * n]
        re, im = re + sr * lr - si * li, im + sr * li + si * lr
        x_ref[rows, 0:n] = re
        x_ref[rows, n:2 * n] = im
        return (jnp.broadcast_to(re[7:8, :], (8, n)), jnp.broadcast_to(im[7:8, :], (8, n)))

    carry = (st_ref[:, 0:n], st_ref[:, n:2 * n])
    for r in range(ts // 8):
        carry = group(r, carry)
    sr, si = carry
    st_ref[:, 0:n] = sr
    st_ref[:, n:2 * n] = si

    y = jnp.dot(x_ref[...].astype(BF16), cblk_ref[...], preferred_element_type=F32)
    y = _gelu(y + d_ref[...] * u)
    gate = jnp.dot(y.astype(BF16), gw_ref[...], preferred_element_type=F32) + gb_ref[...]
    o_ref[...] = y * _sigmoid(gate)


def _ssm_tables(lam_re, lam_im, log_dt, b_re, b_im, c_re, c_im):
    g, p = lam_re.shape
    hch = b_re.shape[2]
    n = g * p
    lr = jnp.minimum(lam_re.astype(F32), -1e-4)
    li = lam_im.astype(F32)
    dt = jnp.exp(log_dt.astype(F32))[:, None]

    def lam_pow(steps):
        mag = jnp.exp(lr * dt * steps)
        return mag * jnp.cos(li * dt * steps), mag * jnp.sin(li * dt * steps)

    br, bi = lam_pow(1.0)
    norm = lr * lr + li * li
    cr = ((br - 1.0) * lr + bi * li) / norm
    ci = (bi * lr - (br - 1.0) * li) / norm
    bre, bim = b_re.astype(F32), b_im.astype(F32)
    bbar_re = cr[..., None] * bre - ci[..., None] * bim
    bbar_im = cr[..., None] * bim + ci[..., None] * bre
    eye = jnp.eye(g, dtype=F32)
    bb_re = jnp.einsum('gph,gk->ghkp', bbar_re, eye).reshape(g * hch, n)
    bb_im = jnp.einsum('gph,gk->ghkp', bbar_im, eye).reshape(g * hch, n)
    bblk = jnp.concatenate([bb_re, bb_im], axis=1).astype(BF16)
    cc_re = jnp.einsum('ghp,gk->kpgh', c_re.astype(F32), eye).reshape(n, g * hch)
    cc_im = jnp.einsum('ghp,gk->kpgh', c_im.astype(F32), eye).reshape(n, g * hch)
    cblk = jnp.concatenate([cc_re, -cc_im], axis=0).astype(BF16)
    t = jnp.arange(8)
    tabs = []
    for d in (1, 2, 4):
        pr, pi = lam_pow(float(d))
        keep = (t >= d)[:, None]
        tabs.append(jnp.concatenate([jnp.where(keep, pr.reshape(1, n), 0.0),
                                     jnp.where(keep, pi.reshape(1, n), 0.0)], axis=1))
    rows = [lam_pow(float(k + 1)) for k in range(8)]
    tabs.append(jnp.concatenate([jnp.stack([r[0].reshape(n) for r in rows]),
                                 jnp.stack([r[1].reshape(n) for r in rows])], axis=1))
    return bblk, cblk, jnp.stack(tabs).astype(F32)


def s5_ssm(proj, batch, seq, col0, bblk, cblk, pw, d_skip, glu_w, glu_b, ts):
    ds = bblk.shape[0]
    n = bblk.shape[1] // 2
    nt = seq // ts
    kern = functools.partial(_ssm_kernel, ts=ts, n=n)
    cb = col0 // ds
    return pl.pallas_call(
        kern,
        grid=(batch, nt),
        in_specs=[pl.BlockSpec((ts, ds), lambda b, c: (b * nt + c, cb)),
                  pl.BlockSpec((ds, 2 * n), lambda b, c: (0, 0)),
                  pl.BlockSpec((4, 8, 2 * n), lambda b, c: (0, 0, 0)),
                  pl.BlockSpec((2 * n, ds), lambda b, c: (0, 0)),
                  pl.BlockSpec((1, ds), lambda b, c: (0, 0)),
                  pl.BlockSpec((ds, ds), lambda b, c: (0, 0)),
                  pl.BlockSpec((1, ds), lambda b, c: (0, 0))],
        out_specs=pl.BlockSpec((ts, ds), lambda b, c: (b * nt + c, 0)),
        out_shape=jax.ShapeDtypeStruct((batch * seq, ds), F32),
        scratch_shapes=[pltpu.VMEM((8, 2 * n), F32), pltpu.VMEM((ts, 2 * n), F32)],
        compiler_params=_params(("parallel", "arbitrary")),
    )(proj, bblk, pw, cblk, d_skip.reshape(1, ds), glu_w.astype(BF16), glu_b.reshape(1, ds))


def _mix_mem_kernel(x_ref, yc_ref, ya_ref, ys_ref, gg_ref, wout_ref, g_ref, wq_ref, kv_ref, wo_ref, o_ref, *, heads):
    x = x_ref[...]
    off = 0
    for y_ref in (yc_ref, ya_ref, ys_ref):
        width = y_ref.shape[1]
        h = _rms(y_ref[...], gg_ref[:, off:off + width]).astype(BF16)
        x = x + jnp.dot(h, wout_ref[off:off + width, :], preferred_element_type=F32)
        off += width
    d = x.shape[1]
    dh = d // heads
    h = _rms(x, g_ref[...]).astype(BF16)
    q = jnp.dot(h, wq_ref[...], preferred_element_type=F32) * (1.0 / math.sqrt(dh))
    outs = []
    for hd in range(heads):
        qh = q[:, hd * dh:(hd + 1) * dh].astype(BF16)
        kh = kv_ref[:, hd * dh:(hd + 1) * dh].astype(BF16)
        vh = kv_ref[:, d + hd * dh:d + (hd + 1) * dh].astype(BF16)
        s = lax.dot_general(qh, kh, (((1,), (1,)), ((), ())), preferred_element_type=F32)
        e = jnp.exp(s - jnp.max(s, axis=-1, keepdims=True))
        p = e / jnp.sum(e, axis=-1, keepdims=True)
        outs.append(jnp.dot(p.astype(BF16), vh, preferred_element_type=F32).astype(BF16))
    o = jnp.concatenate(outs, axis=-1)
    o_ref[...] = x + jnp.dot(o, wo_ref[...], preferred_element_type=F32)


def mix_mem(x, yc, ya, ys, g_grp, w_out, g_mem, wq, memkv, wo, seq, n_mem, tm):
    m, d = x.shape
    dc, da, dsm = yc.shape[1], ya.shape[1], ys.shape[1]
    tiles_per_seq = seq // tm
    kern = functools.partial(_mix_mem_kernel, heads=MEM_HEADS)
    row_blk = lambda i: (i, 0)
    whole = lambda i: (0, 0)
    return pl.pallas_call(
        kern,
        grid=(m // tm,),
        in_specs=[pl.BlockSpec((tm, d), row_blk),
                  pl.BlockSpec((tm, dc), row_blk),
                  pl.BlockSpec((tm, da), row_blk),
                  pl.BlockSpec((tm, dsm), row_blk),
                  pl.BlockSpec((1, dc + da + dsm), whole),
                  pl.BlockSpec((dc + da + dsm, d), whole),
                  pl.BlockSpec((1, d), whole),
                  pl.BlockSpec((d, d), whole),
                  pl.BlockSpec((n_mem, 2 * d), lambda i: (i // tiles_per_seq, 0)),
                  pl.BlockSpec((d, d), whole)],
        out_specs=pl.BlockSpec((tm, d), row_blk),
        out_shape=jax.ShapeDtypeStruct((m, d), F32),
        compiler_params=_params(("parallel",)),
    )(x, yc, ya, ys, g_grp.reshape(1, -1), w_out, g_mem.reshape(1, d), wq, memkv, wo)


def _sort16_network():
    def merge(lo, hi, r):
        step = r * 2
        if step < hi - lo:
            yield from merge(lo, hi, step)
            yield from merge(lo + r, hi, step)
            yield from [(i, i + r) for i in range(lo + r, hi - r, step)]
        else:
            yield (lo, lo + r)

    def sort(lo, hi):
        if hi - lo >= 1:
            mid = lo + (hi - lo) // 2
            yield from sort(lo, mid)
            yield from sort(mid + 1, hi)
            yield from merge(lo, hi, 1)

    return tuple(sort(0, PEER_TOPK - 1))


_SORT16 = _sort16_network()


def _exchange(v, i, j):
    v[i], v[j] = jnp.maximum(v[i], v[j]), jnp.minimum(v[i], v[j])


def _top16_sorted(x):
    v = [x[i * SUBLANES:(i + 1) * SUBLANES, :] for i in range(PEER_TOPK)]
    for i, j in _SORT16:
        _exchange(v, i, j)
    for shift in (4, 2, 1):
        v = [jnp.maximum(v[i], pltpu.roll(v[PEER_TOPK - 1 - i], shift, 0)) for i in range(PEER_TOPK)]
        for d in (8, 4, 2, 1):
            for i in range(PEER_TOPK):
                if not i & d:
                    _exchange(v, i, i + d)
    return v


def _peer_route_kernel(x_ref, g_ref, wq_ref, keys_ref, hb_ref, rank2_ref, e2_ref, cnt_ref, c1_ref,
                       q_scr, top_scr, *, tp):
    nk = PEER_KEYS
    h = _rms(x_ref[...], g_ref[...]).astype(BF16)
    hb_ref[...] = h
    q = jnp.dot(h, wq_ref[...], preferred_element_type=F32)
    for c in range(2 * PEER_HEADS):
        q_scr[c] = q[:, c * nk:(c + 1) * nk].astype(BF16)

    def head(hd, _):
        nt = (((1,), (1,)), ((), ()))
        s1 = lax.dot_general(keys_ref[2 * hd], q_scr[2 * hd], nt, preferred_element_type=F32)
        s2 = lax.dot_general(keys_ref[2 * hd + 1], q_scr[2 * hd + 1], nt, preferred_element_type=F32)
        va = _top16_sorted(s1)
        vb = _top16_sorted(s2)
        for r in range(PEER_TOPK):
            top_scr[0, r:r + 1, :] = va[r][0:1, :]
            top_scr[1, r:r + 1, :] = vb[r][0:1, :]
        a = top_scr[0]
        b = top_scr[1]
        cands = [a[0:1, :] + b]
        for j in range(1, 8):
            cands.append(a[j:j + 1, :] + b[0:8, :])
        cands.append(a[8:16, :] + b[0:1, :])
        cands.append(jnp.full((nk - PAIR_CANDIDATES, tp), -jnp.inf, F32))
        best = _top16_sorted(jnp.concatenate(cands, axis=0))
        thr = best[PEER_TOPK - 1][0:1, :]
        zsum = jnp.zeros_like(best[0])
        for r in range(PEER_TOPK):
            zsum = zsum + jnp.exp(best[r] - best[0])
        groups = (PEER_TOPK, SUBLANES, tp)
        s1g = s1.reshape(groups)
        s2g = s2.reshape(groups)
        cnt = jnp.zeros(groups, F32)
        for j in range(PEER_TOPK):
            cj = jnp.sum(((a[j:j + 1, :] + b) >= thr).astype(F32), axis=0, keepdims=True)
            cnt = jnp.where(s1g == va[j][None], cj[None], cnt)
        rank2 = jnp.full(groups, RANK_NONE, F32)
        for r in reversed(range(PEER_TOPK)):
            rank2 = jnp.where(s2g == vb[r][None], float(r), rank2)
        cnt_ref[hd] = cnt.reshape(nk, tp)
        c1_ref[hd] = (jnp.exp(s1g - va[0][None]) / zsum[None]).reshape(nk, tp)
        packed = (nk // BF16_ROWS, BF16_ROWS, tp)
        rank2_ref[hd] = rank2.reshape(packed).astype(BF16)
        e2_ref[hd] = jnp.exp(s2g - vb[0][None]).reshape(packed).astype(BF16)
        return 0

    lax.fori_loop(0, PEER_HEADS, head, 0)


def peer_route(x, g, wq, keys, tp):
    m, d = x.shape
    nq = wq.shape[1]
    nk = PEER_KEYS
    kern = functools.partial(_peer_route_kernel, tp=tp)
    tok_blk = lambda i: (0, 0, i)
    packed_blk = lambda i: (0, 0, 0, i)
    packed = (PEER_HEADS, nk // BF16_ROWS, BF16_ROWS)
    return pl.pallas_call(
        kern,
        grid=(m // tp,),
        in_specs=[pl.BlockSpec((tp, d), lambda i: (i, 0)),
                  pl.BlockSpec((1, d), lambda i: (0, 0)),
                  pl.BlockSpec((d, nq), lambda i: (0, 0)),
                  pl.BlockSpec((2 * PEER_HEADS, nk, nk), lambda i: (0, 0, 0))],
        out_specs=[pl.BlockSpec((tp, d), lambda i: (i, 0)),
                   pl.BlockSpec(packed + (tp,), packed_blk),
                   pl.BlockSpec(packed + (tp,), packed_blk),
                   pl.BlockSpec((PEER_HEADS, nk, tp), tok_blk),
                   pl.BlockSpec((PEER_HEADS, nk, tp), tok_blk)],
        out_shape=[jax.ShapeDtypeStruct((m, d), BF16),
                   jax.ShapeDtypeStruct(packed + (m,), BF16),
                   jax.ShapeDtypeStruct(packed + (m,), BF16),
                   jax.ShapeDtypeStruct((PEER_HEADS, nk, m), F32),
                   jax.ShapeDtypeStruct((PEER_HEADS, nk, m), F32)],
        scratch_shapes=[pltpu.VMEM((2 * PEER_HEADS, tp, nk), BF16),
                        pltpu.VMEM((2, PEER_TOPK, tp), F32)],
        compiler_params=_params(("parallel",)),
    )(x, g.reshape(1, d), wq, keys)


def _peer_experts_kernel(x_ref, hb_ref, rank2_ref, e2_ref, cnt_ref, c1_ref, u_ref, vt_ref, fg_ref, o_ref,
                         acc_ref, hw_ref, *, blocks, final_norm):
    e = pl.program_id(1)
    nk = PEER_KEYS
    tt = hb_ref.shape[0]
    tiles = nk // BF16_ROWS

    @pl.when(e == 0)
    def _():
        acc_ref[...] = jnp.zeros_like(acc_ref)

    pre = lax.dot_general(u_ref[...], hb_ref[...], (((1,), (1,)), ((), ())), preferred_element_type=F32)
    for j in range(blocks):
        w = None
        for hd in range(PEER_HEADS):
            cnt = jnp.broadcast_to(cnt_ref[hd, j:j + 1, :], (BF16_ROWS, tt)).astype(BF16)
            c1 = jnp.broadcast_to(c1_ref[hd, j:j + 1, :], (BF16_ROWS, tt)).astype(BF16)
            term = jnp.where(rank2_ref[hd] < cnt[None], e2_ref[hd], jnp.zeros((), BF16)) * c1[None]
            w = term if w is None else w + term
        act = _gelu(pre[j * nk:(j + 1) * nk, :].reshape(tiles, BF16_ROWS, tt).astype(BF16))
        hw_ref[j * tiles:(j + 1) * tiles] = act * w
    hw = hw_ref[...].reshape(blocks * nk, tt)
    acc_ref[...] += jnp.dot(vt_ref[...], hw, preferred_element_type=F32)

    @pl.when(e == pl.num_programs(1) - 1)
    def _():
        out = x_ref[...] + acc_ref[...].T
        o_ref[...] = _rms(out, fg_ref[...]) if final_norm else out


def peer_experts(x, hb, rank2, e2, cnt, c1, u_tabs, vt_tabs, layer, final_g, final_norm, tt, blocks):
    m, d = x.shape
    n_exp = u_tabs.shape[1]
    nk = PEER_KEYS
    te = blocks * nk
    kern = functools.partial(_peer_experts_kernel, blocks=blocks, final_norm=final_norm)
    packed = (PEER_HEADS, nk // BF16_ROWS, BF16_ROWS, tt)
    return pl.pallas_call(
        kern,
        grid=(m // tt, n_exp // te),
        in_specs=[pl.BlockSpec((tt, d), lambda t, e: (t, 0)),
                  pl.BlockSpec((tt, d), lambda t, e: (t, 0)),
                  pl.BlockSpec(packed, lambda t, e: (0, 0, 0, t)),
                  pl.BlockSpec(packed, lambda t, e: (0, 0, 0, t)),
                  pl.BlockSpec((PEER_HEADS, blocks, tt), lambda t, e: (0, e, t)),
                  pl.BlockSpec((PEER_HEADS, blocks, tt), lambda t, e: (0, e, t)),
                  pl.BlockSpec((None, te, d), lambda t, e: (layer, e, 0)),
                  pl.BlockSpec((None, d, te), lambda t, e: (layer, 0, e)),
                  pl.BlockSpec((1, d), lambda t, e: (0, 0))],
        out_specs=pl.BlockSpec((tt, d), lambda t, e: (t, 0)),
        out_shape=jax.ShapeDtypeStruct((m, d), F32),
        scratch_shapes=[pltpu.VMEM((d, tt), F32), pltpu.VMEM((te // BF16_ROWS, BF16_ROWS, tt), BF16)],
        compiler_params=_params(("parallel", "arbitrary")),
    )(x, hb, rank2, e2, cnt, c1, u_tabs, vt_tabs, final_g.reshape(1, d))


def _tiles(seq, n_mem):
    return dict(
        proj=min(512, seq), conv=min(512, seq), sba=min(256, seq), ssm=min(512, seq), mix=min(512, seq),
        memkv=min(256, n_mem), route=min(512, seq), experts=min(1024, seq), blocks=8)


def kernel(x, mem, norm_mix_g, w_in, conv_w, conv_b, conv_ln_g, conv_ln_b, ssm_lambda_re, ssm_lambda_im, ssm_log_dt, ssm_b_re, ssm_b_im, ssm_c_re, ssm_c_im, ssm_d, ssm_glu_w, ssm_glu_b, grp_norm_g, w_out, norm_mem_g, mem_norm_g, w_mq, w_mkv, w_mo, norm_ffn_g, peer_wq, peer_sub_keys, peer_u, peer_v, final_norm_g):
    batch, seq, d = x.shape
    n_mem = mem.shape[1]
    depth = w_in.shape[0]
    d_conv = conv_w.shape[2]
    d_ssm = ssm_d.shape[1]
    d_attn = (w_in.shape[2] - 2 * d_conv - d_ssm) // 3
    t = _tiles(seq, n_mem)

    xf = x.reshape(batch * seq, d)
    memf = mem.reshape(batch * n_mem, d)
    u_tabs = peer_u.astype(BF16)
    vt_tabs = jnp.swapaxes(peer_v, 1, 2).astype(BF16)
    for l in range(depth):
        proj = norm_matmul(xf, norm_mix_g[l], w_in[l].astype(BF16), t['proj'])
        yc = conformer_conv(proj, conv_w[l], conv_b[l], conv_ln_g[l], conv_ln_b[l], seq, t['conv'])
        ya = stick_breaking_attention(proj, batch, seq, 2 * d_conv, d_attn, t['sba'])
        bblk, cblk, pw = _ssm_tables(ssm_lambda_re[l], ssm_lambda_im[l], ssm_log_dt[l], ssm_b_re[l], ssm_b_im[l],
                                     ssm_c_re[l], ssm_c_im[l])
        ys = s5_ssm(proj, batch, seq, 2 * d_conv + 3 * d_attn, bblk, cblk, pw, ssm_d[l], ssm_glu_w[l], ssm_glu_b[l],
                    t['ssm'])
        memkv = norm_matmul(memf, mem_norm_g[l], w_mkv[l].astype(BF16), t['memkv'])
        xf = mix_mem(xf, yc, ya, ys, grp_norm_g[l], w_out[l].astype(BF16), norm_mem_g[l], w_mq[l].astype(BF16), memkv,
                     w_mo[l].astype(BF16), seq, n_mem, t['mix'])

        keys = peer_sub_keys[l].reshape(2 * PEER_HEADS, PEER_KEYS, -1).astype(BF16)
        hb, rank2, e2, cnt, c1 = peer_route(xf, norm_ffn_g[l], peer_wq[l].astype(BF16), keys, t['route'])
        xf = peer_experts(xf, hb, rank2, e2, cnt, c1, u_tabs, vt_tabs, l, final_norm_g, l == depth - 1,
                          t['experts'], t['blocks'])
    return xf.reshape(batch, seq, d)
```

```python
import functools
import math

import jax
import jax.numpy as jnp
from jax import lax
from jax.experimental import pallas as pl
from jax.experimental.pallas import tpu as pltpu

F32 = jnp.float32
BF16 = jnp.bfloat16
EPS = 1e-6

CONV_WIDTH = 31
SUBLANES = 8
CONV_HALO = 32
ATTN_HEAD_DIM = 64
SBA_HEADS_PER_STEP = 4
LOG2E = 1.4426950408889634
UNDERFLOW_LOG2 = -160.0
SSM_GROUP = 16
SSM_STATE = 64
MEM_HEADS = 4
PEER_HEADS = 8
PEER_KEYS = 128
PEER_TOPK = 16
PAIR_CANDIDATES = 80
RANK_NONE = 99.0
BF16_ROWS = 16
VMEM_LIMIT = 50 * 1024 * 1024


def _params(sem, vmem=VMEM_LIMIT):
    return pltpu.CompilerParams(dimension_semantics=sem, vmem_limit_bytes=vmem)


def _rms(x, g):
    ms = jnp.mean(x * x, axis=-1, keepdims=True)
    return x * lax.rsqrt(ms + EPS) * g


def _gelu(x):
    c = math.sqrt(2.0 / math.pi)
    return 0.5 * x * (1.0 + jnp.tanh(c * (x + 0.044715 * (x * x * x))))


def _sigmoid(x):
    return 1.0 / (1.0 + jnp.exp(-x))


def _norm_matmul_kernel(x_ref, g_ref, w_ref, o_ref):
    h = _rms(x_ref[...], g_ref[...]).astype(BF16)
    o_ref[...] = jnp.dot(h, w_ref[...], preferred_element_type=F32)


def norm_matmul(x, g, w, tm):
    m, d = x.shape
    n = w.shape[1]
    return pl.pallas_call(
        _norm_matmul_kernel,
        grid=(m // tm,),
        in_specs=[pl.BlockSpec((tm, d), lambda i: (i, 0)),
                  pl.BlockSpec((1, d), lambda i: (0, 0)),
                  pl.BlockSpec((d, n), lambda i: (0, 0))],
        out_specs=pl.BlockSpec((tm, n), lambda i: (i, 0)),
        out_shape=jax.ShapeDtypeStruct((m, n), F32),
        compiler_params=_params(("parallel",)),
    )(x, g.reshape(1, d), w)


def _conv_kernel(cur_ref, halo_ref, w_ref, b_ref, lng_ref, lnb_ref, o_ref, ext_ref, phase_ref, *, ts, tiles_per_seq,
                 dc):
    i = pl.program_id(0)
    cur = cur_ref[...]
    halo = halo_ref[...]
    y = cur[:, :dc] * _sigmoid(cur[:, dc:])
    yh = halo[:, :dc] * _sigmoid(halo[:, dc:])
    yh = jnp.where(i % tiles_per_seq == 0, 0.0, yh)
    ext_ref[0:CONV_HALO, :] = yh
    ext_ref[CONV_HALO:, :] = y
    span = ts + CONV_HALO - SUBLANES
    for r in range(1, SUBLANES):
        phase_ref[r - 1, 0:span, :] = ext_ref[r:r + span, :]
    acc = jnp.broadcast_to(b_ref[...], (ts, dc))
    for k in range(CONV_WIDTH):
        off = CONV_HALO - (CONV_WIDTH - 1) + k
        phase, base = off % SUBLANES, off - off % SUBLANES
        window = ext_ref[base:base + ts, :] if phase == 0 else phase_ref[phase - 1, base:base + ts, :]
        acc = acc + w_ref[k:k + 1, :] * window
    mu = jnp.mean(acc, axis=-1, keepdims=True)
    cen = acc - mu
    var = jnp.mean(cen * cen, axis=-1, keepdims=True)
    z = cen * lax.rsqrt(var + EPS) * lng_ref[...] + lnb_ref[...]
    o_ref[...] = z * _sigmoid(z)


def conformer_conv(proj, conv_w, conv_b, ln_g, ln_b, seq, ts):
    m = proj.shape[0]
    dc = conv_w.shape[1]
    kern = functools.partial(_conv_kernel, ts=ts, tiles_per_seq=seq // ts, dc=dc)
    halo_blocks = ts // CONV_HALO
    return pl.pallas_call(
        kern,
        grid=(m // ts,),
        in_specs=[pl.BlockSpec((ts, 2 * dc), lambda i: (i, 0)),
                  pl.BlockSpec((CONV_HALO, 2 * dc), lambda i: (jnp.maximum(i * halo_blocks - 1, 0), 0)),
                  pl.BlockSpec((CONV_WIDTH, dc), lambda i: (0, 0)),
                  pl.BlockSpec((1, dc), lambda i: (0, 0)),
                  pl.BlockSpec((1, dc), lambda i: (0, 0)),
                  pl.BlockSpec((1, dc), lambda i: (0, 0))],
        out_specs=pl.BlockSpec((ts, dc), lambda i: (i, 0)),
        out_shape=jax.ShapeDtypeStruct((m, dc), F32),
        scratch_shapes=[pltpu.VMEM((ts + CONV_HALO, dc), F32),
                        pltpu.VMEM((SUBLANES - 1, ts + CONV_HALO - SUBLANES, dc), F32)],
        compiler_params=_params(("parallel",)),
    )(proj, proj, conv_w, conv_b.reshape(1, dc), ln_g.reshape(1, dc), ln_b.reshape(1, dc))


def _sba_kernel(q_ref, k_ref, v_ref, o_ref, *, tq, scale):
    i = pl.program_id(2)
    dh = ATTN_HEAD_DIM
    row = lax.broadcasted_iota(jnp.int32, (tq, tq), 0)
    col = lax.broadcasted_iota(jnp.int32, (tq, tq), 1)
    causal = col < row
    suffix = (row > col).astype(BF16)

    heads = q_ref.shape[1] // dh
    lanes = [slice(hh * dh, (hh + 1) * dh) for hh in range(heads)]
    qs = [(q_ref[:, ln] * (scale * LOG2E)).astype(BF16) for ln in lanes]

    def block(j, carry, diagonal):
        start = pl.multiple_of(j * tq, tq)
        hs = range(heads)
        kj = [k_ref[pl.ds(start, tq), lanes[h]].astype(BF16) for h in hs]
        vj = [v_ref[pl.ds(start, tq), lanes[h]].astype(BF16) for h in hs]
        z = [lax.dot_general(qs[h], kj[h], (((1,), (1,)), ((), ())), preferred_element_type=F32) for h in hs]
        log_beta, log_stay, later = [], [], []
        for h in hs:
            sp = jnp.log2(1.0 + jnp.exp2(-jnp.abs(z[h])))
            log_beta.append(jnp.minimum(z[h], 0.0) - sp)
            stay = log_beta[h] - z[h]
            log_stay.append(jnp.where(causal, stay, 0.0) if diagonal else stay)
        for h in hs:
            later.append(jnp.dot(log_stay[h].astype(BF16), suffix, preferred_element_type=F32))
        out = []
        for h in hs:
            acc, run = carry[h]
            a = jnp.exp2(log_beta[h] + later[h] + run)
            if diagonal:
                a = jnp.where(causal, a, 0.0)
            acc = acc + jnp.dot(a.astype(BF16), vj[h], preferred_element_type=F32)
            run = run + later[h][:, 0:1] + log_stay[h][:, 0:1]
            out.append((acc, run))
        return tuple(out)

    def unfinished(c):
        return jnp.logical_and(c[0] < i, c[1])

    def sweep(c):
        jj, _, carry = c
        carry = block(i - 1 - jj, carry, False)
        top = carry[0][1]
        for h in range(1, heads):
            top = jnp.maximum(top, carry[h][1])
        return jj + 1, jnp.max(top) > UNDERFLOW_LOG2, carry

    carry = tuple((jnp.zeros((tq, dh), F32), jnp.zeros((tq, 1), F32)) for _ in range(heads))
    carry = block(i, carry, True)
    gated = tuple((acc, jnp.where(i > 0, run, -jnp.inf)) for acc, run in carry)
    carry = block(jnp.maximum(i - 1, 0), gated, False)
    top = carry[0][1]
    for h in range(1, heads):
        top = jnp.maximum(top, carry[h][1])
    _, _, carry = lax.while_loop(unfinished, sweep, (jnp.int32(1), jnp.max(top) > UNDERFLOW_LOG2, carry))
    for hh in range(heads):
        o_ref[:, lanes[hh]] = carry[hh][0]


def stick_breaking_attention(proj, batch, seq, col0, d_attn, tq):
    lane_blk = SBA_HEADS_PER_STEP * ATTN_HEAD_DIM
    hp = d_attn // lane_blk
    qb, kb, vb = col0 // lane_blk, (col0 + d_attn) // lane_blk, (col0 + 2 * d_attn) // lane_blk
    nq = seq // tq
    kern = functools.partial(_sba_kernel, tq=tq, scale=1.0 / math.sqrt(ATTN_HEAD_DIM))
    return pl.pallas_call(
        kern,
        grid=(batch, hp, nq),
        in_specs=[pl.BlockSpec((tq, lane_blk), lambda b, h, i: (b * nq + i, qb + h)),
                  pl.BlockSpec((seq, lane_blk), lambda b, h, i: (b, kb + h)),
                  pl.BlockSpec((seq, lane_blk), lambda b, h, i: (b, vb + h))],
        out_specs=pl.BlockSpec((tq, lane_blk), lambda b, h, i: (b * nq + i, h)),
        out_shape=jax.ShapeDtypeStruct((batch * seq, d_attn), F32),
        compiler_params=_params(("parallel", "parallel", "arbitrary")),
    )(proj, proj, proj)


def _ssm_kernel(u_ref, bblk_ref, pw_ref, cblk_ref, d_ref, gw_ref, gb_ref, o_ref, st_ref, x_ref, *, ts, n):
    c = pl.program_id(1)

    @pl.when(c == 0)
    def _():
        st_ref[...] = jnp.zeros_like(st_ref)

    u = u_ref[...]
    x_ref[...] = jnp.dot(u.astype(BF16), bblk_ref[...], preferred_element_type=F32)

    def group(r, carry):
        sr, si = carry
        rows = pl.ds(r * SUBLANES, SUBLANES)
        re = x_ref[rows, 0:n]
        im = x_ref[rows, n:2 * n]
        for lvl, d in enumerate((1, 2, 4)):
            lr = pw_ref[lvl, :, 0:n]
            li = pw_ref[lvl, :, n:2 * n]
            pr = pltpu.roll(re, d, 0)
            pi = pltpu.roll(im, d, 0)
            re, im = re + pr * lr - pi * li, im + pr * li + pi * lr
        lr = pw_ref[3, :, 0:n]
        li = pw_ref[3, :, n:2 * n]
        re, im = re + sr * lr - si * li, im + sr * li + si * lr
        x_ref[rows, 0:n] = re
        x_ref[rows, n:2 * n] = im
        last = slice(SUBLANES - 1, SUBLANES)
        return (jnp.broadcast_to(re[last, :], (SUBLANES, n)), jnp.broadcast_to(im[last, :], (SUBLANES, n)))

    carry = (st_ref[:, 0:n], st_ref[:, n:2 * n])
    for r in range(ts // SUBLANES):
        carry = group(r, carry)
    sr, si = carry
    st_ref[:, 0:n] = sr
    st_ref[:, n:2 * n] = si

    y = jnp.dot(x_ref[...].astype(BF16), cblk_ref[...], preferred_element_type=F32)
    y = _gelu(y + d_ref[...] * u)
    gate = jnp.dot(y.astype(BF16), gw_ref[...], preferred_element_type=F32) + gb_ref[...]
    o_ref[...] = y * _sigmoid(gate)


def _ssm_tables(lam_re, lam_im, log_dt, b_re, b_im, c_re, c_im):
    g, p = lam_re.shape
    hch = b_re.shape[2]
    n = g * p
    lr = jnp.minimum(lam_re.astype(F32), -1e-4)
    li = lam_im.astype(F32)
    dt = jnp.exp(log_dt.astype(F32))[:, None]

    def lam_pow(steps):
        mag = jnp.exp(lr * dt * steps)
        return mag * jnp.cos(li * dt * steps), mag * jnp.sin(li * dt * steps)

    br, bi = lam_pow(1.0)
    norm = lr * lr + li * li
    cr = ((br - 1.0) * lr + bi * li) / norm
    ci = (bi * lr - (br - 1.0) * li) / norm
    bre, bim = b_re.astype(F32), b_im.astype(F32)
    bbar_re = cr[..., None] * bre - ci[..., None] * bim
    bbar_im = cr[..., None] * bim + ci[..., None] * bre
    eye = jnp.eye(g, dtype=F32)
    bb_re = jnp.einsum('gph,gk->ghkp', bbar_re, eye).reshape(g * hch, n)
    bb_im = jnp.einsum('gph,gk->ghkp', bbar_im, eye).reshape(g * hch, n)
    bblk = jnp.concatenate([bb_re, bb_im], axis=1).astype(BF16)
    cc_re = jnp.einsum('ghp,gk->kpgh', c_re.astype(F32), eye).reshape(n, g * hch)
    cc_im = jnp.einsum('ghp,gk->kpgh', c_im.astype(F32), eye).reshape(n, g * hch)
    cblk = jnp.concatenate([cc_re, -cc_im], axis=0).astype(BF16)
    t = jnp.arange(SUBLANES)
    tabs = []
    for d in (1, 2, 4):
        pr, pi = lam_pow(float(d))
        keep = (t >= d)[:, None]
        tabs.append(jnp.concatenate([jnp.where(keep, pr.reshape(1, n), 0.0),
                                     jnp.where(keep, pi.reshape(1, n), 0.0)], axis=1))
    rows = [lam_pow(float(k + 1)) for k in range(SUBLANES)]
    tabs.append(jnp.concatenate([jnp.stack([r[0].reshape(n) for r in rows]),
                                 jnp.stack([r[1].reshape(n) for r in rows])], axis=1))
    return bblk, cblk, jnp.stack(tabs).astype(F32)


def s5_ssm(proj, batch, seq, col0, bblk, cblk, pw, d_skip, glu_w, glu_b, ts):
    ds = bblk.shape[0]
    n = bblk.shape[1] // 2
    nt = seq // ts
    kern = functools.partial(_ssm_kernel, ts=ts, n=n)
    cb = col0 // ds
    return pl.pallas_call(
        kern,
        grid=(batch, nt),
        in_specs=[pl.BlockSpec((ts, ds), lambda b, c: (b * nt + c, cb)),
                  pl.BlockSpec((ds, 2 * n), lambda b, c: (0, 0)),
                  pl.BlockSpec((4, SUBLANES, 2 * n), lambda b, c: (0, 0, 0)),
                  pl.BlockSpec((2 * n, ds), lambda b, c: (0, 0)),
                  pl.BlockSpec((1, ds), lambda b, c: (0, 0)),
                  pl.BlockSpec((ds, ds), lambda b, c: (0, 0)),
                  pl.BlockSpec((1, ds), lambda b, c: (0, 0))],
        out_specs=pl.BlockSpec((ts, ds), lambda b, c: (b * nt + c, 0)),
        out_shape=jax.ShapeDtypeStruct((batch * seq, ds), F32),
        scratch_shapes=[pltpu.VMEM((SUBLANES, 2 * n), F32), pltpu.VMEM((ts, 2 * n), F32)],
        compiler_params=_params(("parallel", "arbitrary")),
    )(proj, bblk, pw, cblk, d_skip.reshape(1, ds), glu_w.astype(BF16), glu_b.reshape(1, ds))


def _mix_mem_kernel(x_ref, yc_ref, ya_ref, ys_ref, gg_ref, wout_ref, g_ref, wq_ref, kv_ref, wo_ref, o_ref, *, heads):
    x = x_ref[...]
    off = 0
    for y_ref in (yc_ref, ya_ref, ys_ref):
        width = y_ref.shape[1]
        h = _rms(y_ref[...], gg_ref[:, off:off + width]).astype(BF16)
        x = x + jnp.dot(h, wout_ref[off:off + width, :], preferred_element_type=F32)
        off += width
    d = x.shape[1]
    dh = d // heads
    h = _rms(x, g_ref[...]).astype(BF16)
    q = jnp.dot(h, wq_ref[...], preferred_element_type=F32) * (1.0 / math.sqrt(dh))
    outs = []
    for hd in range(heads):
        qh = q[:, hd * dh:(hd + 1) * dh].astype(BF16)
        kh = kv_ref[:, hd * dh:(hd + 1) * dh].astype(BF16)
        vh = kv_ref[:, d + hd * dh:d + (hd + 1) * dh].astype(BF16)
        s = lax.dot_general(qh, kh, (((1,), (1,)), ((), ())), preferred_element_type=F32)
        e = jnp.exp(s - jnp.max(s, axis=-1, keepdims=True))
        p = e / jnp.sum(e, axis=-1, keepdims=True)
        outs.append(jnp.dot(p.astype(BF16), vh, preferred_element_type=F32).astype(BF16))
    o = jnp.concatenate(outs, axis=-1)
    o_ref[...] = x + jnp.dot(o, wo_ref[...], preferred_element_type=F32)


def mix_mem(x, yc, ya, ys, g_grp, w_out, g_mem, wq, memkv, wo, seq, n_mem, tm):
    m, d = x.shape
    dc, da, dsm = yc.shape[1], ya.shape[1], ys.shape[1]
    tiles_per_seq = seq // tm
    kern = functools.partial(_mix_mem_kernel, heads=MEM_HEADS)
    row_blk = lambda i: (i, 0)
    whole = lambda i: (0, 0)
    return pl.pallas_call(
        kern,
        grid=(m // tm,),
        in_specs=[pl.BlockSpec((tm, d), row_blk),
                  pl.BlockSpec((tm, dc), row_blk),
                  pl.BlockSpec((tm, da), row_blk),
                  pl.BlockSpec((tm, dsm), row_blk),
                  pl.BlockSpec((1, dc + da + dsm), whole),
                  pl.BlockSpec((dc + da + dsm, d), whole),
                  pl.BlockSpec((1, d), whole),
                  pl.BlockSpec((d, d), whole),
                  pl.BlockSpec((n_mem, 2 * d), lambda i: (i // tiles_per_seq, 0)),
                  pl.BlockSpec((d, d), whole)],
        out_specs=pl.BlockSpec((tm, d), row_blk),
        out_shape=jax.ShapeDtypeStruct((m, d), F32),
        compiler_params=_params(("parallel",)),
    )(x, yc, ya, ys, g_grp.reshape(1, -1), w_out, g_mem.reshape(1, d), wq, memkv, wo)


def _sort16_network():
    def merge(lo, hi, r):
        step = r * 2
        if step < hi - lo:
            yield from merge(lo, hi, step)
            yield from merge(lo + r, hi, step)
            yield from [(i, i + r) for i in range(lo + r, hi - r, step)]
        else:
            yield (lo, lo + r)

    def sort(lo, hi):
        if hi - lo >= 1:
            mid = lo + (hi - lo) // 2
            yield from sort(lo, mid)
            yield from sort(mid + 1, hi)
            yield from merge(lo, hi, 1)

    return tuple(sort(0, PEER_TOPK - 1))


_SORT16 = _sort16_network()


def _exchange(v, i, j):
    v[i], v[j] = jnp.maximum(v[i], v[j]), jnp.minimum(v[i], v[j])


def _top16_sorted(x):
    v = [x[i * SUBLANES:(i + 1) * SUBLANES, :] for i in range(PEER_TOPK)]
    for i, j in _SORT16:
        _exchange(v, i, j)
    for shift in (4, 2, 1):
        v = [jnp.maximum(v[i], pltpu.roll(v[PEER_TOPK - 1 - i], shift, 0)) for i in range(PEER_TOPK)]
        for d in (8, 4, 2, 1):
            for i in range(PEER_TOPK):
                if not i & d:
                    _exchange(v, i, i + d)
    return v


def _peer_route_kernel(x_ref, g_ref, wq_ref, keys_ref, hb_ref, rank2_ref, e2_ref, cnt_ref, c1_ref,
                       q_scr, top_scr, *, tp):
    nk = PEER_KEYS
    h = _rms(x_ref[...], g_ref[...]).astype(BF16)
    hb_ref[...] = h
    q = jnp.dot(h, wq_ref[...], preferred_element_type=F32)
    for c in range(2 * PEER_HEADS):
        q_scr[c] = q[:, c * nk:(c + 1) * nk].astype(BF16)

    def head(hd, _):
        nt = (((1,), (1,)), ((), ()))
        s1 = lax.dot_general(keys_ref[2 * hd], q_scr[2 * hd], nt, preferred_element_type=F32)
        s2 = lax.dot_general(keys_ref[2 * hd + 1], q_scr[2 * hd + 1], nt, preferred_element_type=F32)
        va = _top16_sorted(s1)
        vb = _top16_sorted(s2)
        for r in range(PEER_TOPK):
            top_scr[0, r:r + 1, :] = va[r][0:1, :]
            top_scr[1, r:r + 1, :] = vb[r][0:1, :]
        a = top_scr[0]
        b = top_scr[1]
        cands = [a[0:1, :] + b]
        for j in range(1, 8):
            cands.append(a[j:j + 1, :] + b[0:8, :])
        cands.append(a[8:16, :] + b[0:1, :])
        cands.append(jnp.full((nk - PAIR_CANDIDATES, tp), -jnp.inf, F32))
        best = _top16_sorted(jnp.concatenate(cands, axis=0))
        thr = best[PEER_TOPK - 1][0:1, :]
        zsum = jnp.zeros_like(best[0])
        for r in range(PEER_TOPK):
            zsum = zsum + jnp.exp(best[r] - best[0])
        groups = (PEER_TOPK, SUBLANES, tp)
        s1g = s1.reshape(groups)
        s2g = s2.reshape(groups)
        cnt = jnp.zeros(groups, F32)
        for j in range(PEER_TOPK):
            cj = jnp.sum(((a[j:j + 1, :] + b) >= thr).astype(F32), axis=0, keepdims=True)
            cnt = jnp.where(s1g == va[j][None], cj[None], cnt)
        rank2 = jnp.full(groups, RANK_NONE, F32)
        for r in reversed(range(PEER_TOPK)):
            rank2 = jnp.where(s2g == vb[r][None], float(r), rank2)
        cnt_ref[hd] = cnt.reshape(nk, tp)
        c1_ref[hd] = (jnp.exp(s1g - va[0][None]) / zsum[None]).reshape(nk, tp)
        packed = (nk // BF16_ROWS, BF16_ROWS, tp)
        rank2_ref[hd] = rank2.reshape(packed).astype(BF16)
        e2_ref[hd] = jnp.exp(s2g - vb[0][None]).reshape(packed).astype(BF16)
        return 0

    lax.fori_loop(0, PEER_HEADS, head, 0)


def peer_route(x, g, wq, keys, tp):
    m, d = x.shape
    nq = wq.shape[1]
    nk = PEER_KEYS
    kern = functools.partial(_peer_route_kernel, tp=tp)
    tok_blk = lambda i: (0, 0, i)
    packed_blk = lambda i: (0, 0, 0, i)
    packed = (PEER_HEADS, nk // BF16_ROWS, BF16_ROWS)
    return pl.pallas_call(
        kern,
        grid=(m // tp,),
        in_specs=[pl.BlockSpec((tp, d), lambda i: (i, 0)),
                  pl.BlockSpec((1, d), lambda i: (0, 0)),
                  pl.BlockSpec((d, nq), lambda i: (0, 0)),
                  pl.BlockSpec((2 * PEER_HEADS, nk, nk), lambda i: (0, 0, 0))],
        out_specs=[pl.BlockSpec((tp, d), lambda i: (i, 0)),
                   pl.BlockSpec(packed + (tp,), packed_blk),
                   pl.BlockSpec(packed + (tp,), packed_blk),
                   pl.BlockSpec((PEER_HEADS, nk, tp), tok_blk),
                   pl.BlockSpec((PEER_HEADS, nk, tp), tok_blk)],
        out_shape=[jax.ShapeDtypeStruct((m, d), BF16),
                   jax.ShapeDtypeStruct(packed + (m,), BF16),
                   jax.ShapeDtypeStruct(packed + (m,), BF16),
                   jax.ShapeDtypeStruct((PEER_HEADS, nk, m), F32),
                   jax.ShapeDtypeStruct((PEER_HEADS, nk, m), F32)],
        scratch_shapes=[pltpu.VMEM((2 * PEER_HEADS, tp, nk), BF16),
                        pltpu.VMEM((2, PEER_TOPK, tp), F32)],
        compiler_params=_params(("parallel",)),
    )(x, g.reshape(1, d), wq, keys)


def _peer_experts_kernel(x_ref, hb_ref, rank2_ref, e2_ref, cnt_ref, c1_ref, u_ref, vt_ref, fg_ref, o_ref,
                         acc_ref, hw_ref, *, blocks, final_norm):
    e = pl.program_id(1)
    nk = PEER_KEYS
    tt = hb_ref.shape[0]
    tiles = nk // BF16_ROWS

    @pl.when(e == 0)
    def _():
        acc_ref[...] = jnp.zeros_like(acc_ref)

    pre = lax.dot_general(u_ref[...], hb_ref[...], (((1,), (1,)), ((), ())), preferred_element_type=F32)
    for j in range(blocks):
        w = None
        for hd in range(PEER_HEADS):
            cnt = jnp.broadcast_to(cnt_ref[hd, j:j + 1, :], (BF16_ROWS, tt)).astype(BF16)
            c1 = jnp.broadcast_to(c1_ref[hd, j:j + 1, :], (BF16_ROWS, tt)).astype(BF16)
            term = jnp.where(rank2_ref[hd] < cnt[None], e2_ref[hd], jnp.zeros((), BF16)) * c1[None]
            w = term if w is None else w + term
        act = _gelu(pre[j * nk:(j + 1) * nk, :].reshape(tiles, BF16_ROWS, tt).astype(BF16))
        hw_ref[j * tiles:(j + 1) * tiles] = act * w
    hw = hw_ref[...].reshape(blocks * nk, tt)
    acc_ref[...] += jnp.dot(vt_ref[...], hw, preferred_element_type=F32)

    @pl.when(e == pl.num_programs(1) - 1)
    def _():
        out = x_ref[...] + acc_ref[...].T
        o_ref[...] = _rms(out, fg_ref[...]) if final_norm else out


def peer_experts(x, hb, rank2, e2, cnt, c1, u_tabs, vt_tabs, layer, final_g, final_norm, tt, blocks):
    m, d = x.shape
    n_exp = u_tabs.shape[1]
    nk = PEER_KEYS
    te = blocks * nk
    kern = functools.partial(_peer_experts_kernel, blocks=blocks, final_norm=final_norm)
    packed = (PEER_HEADS, nk // BF16_ROWS, BF16_ROWS, tt)
    return pl.pallas_call(
        kern,
        grid=(m // tt, n_exp // te),
        in_specs=[pl.BlockSpec((tt, d), lambda t, e: (t, 0)),
                  pl.BlockSpec((tt, d), lambda t, e: (t, 0)),
                  pl.BlockSpec(packed, lambda t, e: (0, 0, 0, t)),
                  pl.BlockSpec(packed, lambda t, e: (0, 0, 0, t)),
                  pl.BlockSpec((PEER_HEADS, blocks, tt), lambda t, e: (0, e, t)),
                  pl.BlockSpec((PEER_HEADS, blocks, tt), lambda t, e: (0, e, t)),
                  pl.BlockSpec((None, te, d), lambda t, e: (layer, e, 0)),
                  pl.BlockSpec((None, d, te), lambda t, e: (layer, 0, e)),
                  pl.BlockSpec((1, d), lambda t, e: (0, 0))],
        out_specs=pl.BlockSpec((tt, d), lambda t, e: (t, 0)),
        out_shape=jax.ShapeDtypeStruct((m, d), F32),
        scratch_shapes=[pltpu.VMEM((d, tt), F32), pltpu.VMEM((te // BF16_ROWS, BF16_ROWS, tt), BF16)],
        compiler_params=_params(("parallel", "arbitrary")),
    )(x, hb, rank2, e2, cnt, c1, u_tabs, vt_tabs, final_g.reshape(1, d))


def _tiles(seq, n_mem):
    return dict(
        proj=min(512, seq), conv=min(512, seq), sba=min(256, seq), ssm=min(512, seq), mix=min(512, seq),
        memkv=min(256, n_mem), route=min(512, seq), experts=min(1024, seq), blocks=8)


def kernel(x, mem, norm_mix_g, w_in, conv_w, conv_b, conv_ln_g, conv_ln_b, ssm_lambda_re, ssm_lambda_im, ssm_log_dt, ssm_b_re, ssm_b_im, ssm_c_re, ssm_c_im, ssm_d, ssm_glu_w, ssm_glu_b, grp_norm_g, w_out, norm_mem_g, mem_norm_g, w_mq, w_mkv, w_mo, norm_ffn_g, peer_wq, peer_sub_keys, peer_u, peer_v, final_norm_g):
    batch, seq, d = x.shape
    n_mem = mem.shape[1]
    depth = w_in.shape[0]
    d_conv = conv_w.shape[2]
    d_ssm = ssm_d.shape[1]
    d_attn = (w_in.shape[2] - 2 * d_conv - d_ssm) // 3
    t = _tiles(seq, n_mem)

    xf = x.reshape(batch * seq, d)
    memf = mem.reshape(batch * n_mem, d)
    u_tabs = peer_u.astype(BF16)
    vt_tabs = jnp.swapaxes(peer_v, 1, 2).astype(BF16)
    for l in range(depth):
        proj = norm_matmul(xf, norm_mix_g[l], w_in[l].astype(BF16), t['proj'])
        yc = conformer_conv(proj, conv_w[l], conv_b[l], conv_ln_g[l], conv_ln_b[l], seq, t['conv'])
        ya = stick_breaking_attention(proj, batch, seq, 2 * d_conv, d_attn, t['sba'])
        bblk, cblk, pw = _ssm_tables(ssm_lambda_re[l], ssm_lambda_im[l], ssm_log_dt[l], ssm_b_re[l], ssm_b_im[l],
                                     ssm_c_re[l], ssm_c_im[l])
        ys = s5_ssm(proj, batch, seq, 2 * d_conv + 3 * d_attn, bblk, cblk, pw, ssm_d[l], ssm_glu_w[l], ssm_glu_b[l],
                    t['ssm'])
        memkv = norm_matmul(memf, mem_norm_g[l], w_mkv[l].astype(BF16), t['memkv'])
        xf = mix_mem(xf, yc, ya, ys, grp_norm_g[l], w_out[l].astype(BF16), norm_mem_g[l], w_mq[l].astype(BF16), memkv,
                     w_mo[l].astype(BF16), seq, n_mem, t['mix'])

        keys = peer_sub_keys[l].reshape(2 * PEER_HEADS, PEER_KEYS, -1).astype(BF16)
        hb, rank2, e2, cnt, c1 = peer_route(xf, norm_ffn_g[l], peer_wq[l].astype(BF16), keys, t['route'])
        xf = peer_experts(xf, hb, rank2, e2, cnt, c1, u_tabs, vt_tabs, l, final_norm_g, l == depth - 1,
                          t['experts'], t['blocks'])
    return xf.reshape(batch, seq, d)
```

```python
import functools
import math

import jax
import jax.numpy as jnp
from jax import lax
from jax.experimental import pallas as pl
from jax.experimental.pallas import tpu as pltpu

F32 = jnp.float32
BF16 = jnp.bfloat16
EPS = 1e-6

CONV_WIDTH = 31
SUBLANES = 8
CONV_HALO = 32
ATTN_HEAD_DIM = 64
SBA_HEADS_PER_STEP = 4
LOG2E = 1.4426950408889634
UNDERFLOW_LOG2 = -160.0
SSM_GROUP = 16
SSM_STATE = 64
MEM_HEADS = 4
PEER_HEADS = 8
PEER_KEYS = 128
PEER_TOPK = 16
PAIR_CANDIDATES = 80
RANK_NONE = 99.0
BF16_ROWS = 16
VMEM_LIMIT = 50 * 1024 * 1024


def _params(sem, vmem=VMEM_LIMIT):
    return pltpu.CompilerParams(dimension_semantics=sem, vmem_limit_bytes=vmem)


def _rms(x, g):
    ms = jnp.mean(x * x, axis=-1, keepdims=True)
    return x * lax.rsqrt(ms + EPS) * g


def _gelu(x):
    c = math.sqrt(2.0 / math.pi)
    return 0.5 * x * (1.0 + jnp.tanh(c * (x + 0.044715 * (x * x * x))))


def _sigmoid(x):
    return 1.0 / (1.0 + jnp.exp(-x))


def _norm_matmul_kernel(x_ref, g_ref, w_ref, o_ref):
    h = _rms(x_ref[...], g_ref[...]).astype(BF16)
    o_ref[...] = jnp.dot(h, w_ref[...], preferred_element_type=F32)


def norm_matmul(x, g, w, tm):
    m, d = x.shape
    n = w.shape[1]
    return pl.pallas_call(
        _norm_matmul_kernel,
        grid=(m // tm,),
        in_specs=[pl.BlockSpec((tm, d), lambda i: (i, 0)),
                  pl.BlockSpec((1, d), lambda i: (0, 0)),
                  pl.BlockSpec((d, n), lambda i: (0, 0))],
        out_specs=pl.BlockSpec((tm, n), lambda i: (i, 0)),
        out_shape=jax.ShapeDtypeStruct((m, n), F32),
        compiler_params=_params(("parallel",)),
    )(x, g.reshape(1, d), w)


def _conv_kernel(cur_ref, halo_ref, w_ref, b_ref, lng_ref, lnb_ref, o_ref, ext_ref, phase_ref, *, ts, tiles_per_seq,
                 dc):
    i = pl.program_id(0)
    cur = cur_ref[...]
    halo = halo_ref[...]
    y = cur[:, :dc] * _sigmoid(cur[:, dc:])
    yh = halo[:, :dc] * _sigmoid(halo[:, dc:])
    yh = jnp.where(i % tiles_per_seq == 0, 0.0, yh)
    ext_ref[0:CONV_HALO, :] = yh
    ext_ref[CONV_HALO:, :] = y
    span = ts + CONV_HALO - SUBLANES
    for r in range(1, SUBLANES):
        phase_ref[r - 1, 0:span, :] = ext_ref[r:r + span, :]
    acc = jnp.broadcast_to(b_ref[...], (ts, dc))
    for k in range(CONV_WIDTH):
        off = CONV_HALO - (CONV_WIDTH - 1) + k
        phase, base = off % SUBLANES, off - off % SUBLANES
        window = ext_ref[base:base + ts, :] if phase == 0 else phase_ref[phase - 1, base:base + ts, :]
        acc = acc + w_ref[k:k + 1, :] * window
    mu = jnp.mean(acc, axis=-1, keepdims=True)
    cen = acc - mu
    var = jnp.mean(cen * cen, axis=-1, keepdims=True)
    z = cen * lax.rsqrt(var + EPS) * lng_ref[...] + lnb_ref[...]
    o_ref[...] = z * _sigmoid(z)


def conformer_conv(proj, conv_w, conv_b, ln_g, ln_b, seq, ts):
    m = proj.shape[0]
    dc = conv_w.shape[1]
    kern = functools.partial(_conv_kernel, ts=ts, tiles_per_seq=seq // ts, dc=dc)
    halo_blocks = ts // CONV_HALO
    return pl.pallas_call(
        kern,
        grid=(m // ts,),
        in_specs=[pl.BlockSpec((ts, 2 * dc), lambda i: (i, 0)),
                  pl.BlockSpec((CONV_HALO, 2 * dc), lambda i: (jnp.maximum(i * halo_blocks - 1, 0), 0)),
                  pl.BlockSpec((CONV_WIDTH, dc), lambda i: (0, 0)),
                  pl.BlockSpec((1, dc), lambda i: (0, 0)),
                  pl.BlockSpec((1, dc), lambda i: (0, 0)),
                  pl.BlockSpec((1, dc), lambda i: (0, 0))],
        out_specs=pl.BlockSpec((ts, dc), lambda i: (i, 0)),
        out_shape=jax.ShapeDtypeStruct((m, dc), F32),
        scratch_shapes=[pltpu.VMEM((ts + CONV_HALO, dc), F32),
                        pltpu.VMEM((SUBLANES - 1, ts + CONV_HALO - SUBLANES, dc), F32)],
        compiler_params=_params(("parallel",)),
    )(proj, proj, conv_w, conv_b.reshape(1, dc), ln_g.reshape(1, dc), ln_b.reshape(1, dc))


def _sba_kernel(q_ref, k_ref, v_ref, o_ref, *, tq, scale):
    i = pl.program_id(2)
    dh = ATTN_HEAD_DIM
    row = lax.broadcasted_iota(jnp.int32, (tq, tq), 0)
    col = lax.broadcasted_iota(jnp.int32, (tq, tq), 1)
    causal = col < row
    suffix = (row > col).astype(BF16)

    heads = q_ref.shape[1] // dh
    lanes = [slice(hh * dh, (hh + 1) * dh) for hh in range(heads)]
    qs = [(q_ref[:, ln] * (scale * LOG2E)).astype(BF16) for ln in lanes]

    def block(j, carry, diagonal):
        start = pl.multiple_of(j * tq, tq)
        hs = range(heads)
        kj = [k_ref[pl.ds(start, tq), lanes[h]].astype(BF16) for h in hs]
        vj = [v_ref[pl.ds(start, tq), lanes[h]].astype(BF16) for h in hs]
        z = [lax.dot_general(qs[h], kj[h], (((1,), (1,)), ((), ())), preferred_element_type=F32) for h in hs]
        log_beta, log_stay, later = [], [], []
        for h in hs:
            sp = jnp.log2(1.0 + jnp.exp2(-jnp.abs(z[h])))
            log_beta.append(jnp.minimum(z[h], 0.0) - sp)
            stay = log_beta[h] - z[h]
            log_stay.append(jnp.where(causal, stay, 0.0) if diagonal else stay)
        for h in hs:
            later.append(jnp.dot(log_stay[h].astype(BF16), suffix, preferred_element_type=F32))
        out = []
        for h in hs:
            acc, run = carry[h]
            a = jnp.exp2(log_beta[h] + later[h] + run)
            if diagonal:
                a = jnp.where(causal, a, 0.0)
            acc = acc + jnp.dot(a.astype(BF16), vj[h], preferred_element_type=F32)
            run = run + later[h][:, 0:1] + log_stay[h][:, 0:1]
            out.append((acc, run))
        return tuple(out)

    def unfinished(c):
        return jnp.logical_and(c[0] < i, c[1])

    def sweep(c):
        jj, _, carry = c
        carry = block(i - 1 - jj, carry, False)
        top = carry[0][1]
        for h in range(1, heads):
            top = jnp.maximum(top, carry[h][1])
        return jj + 1, jnp.max(top) > UNDERFLOW_LOG2, carry

    carry = tuple((jnp.zeros((tq, dh), F32), jnp.zeros((tq, 1), F32)) for _ in range(heads))
    carry = block(i, carry, True)
    gated = tuple((acc, jnp.where(i > 0, run, -jnp.inf)) for acc, run in carry)
    carry = block(jnp.maximum(i - 1, 0), gated, False)
    top = carry[0][1]
    for h in range(1, heads):
        top = jnp.maximum(top, carry[h][1])
    _, _, carry = lax.while_loop(unfinished, sweep, (jnp.int32(1), jnp.max(top) > UNDERFLOW_LOG2, carry))
    for hh in range(heads):
        o_ref[:, lanes[hh]] = carry[hh][0]


def stick_breaking_attention(proj, batch, seq, col0, d_attn, tq):
    lane_blk = SBA_HEADS_PER_STEP * ATTN_HEAD_DIM
    hp = d_attn // lane_blk
    qb, kb, vb = col0 // lane_blk, (col0 + d_attn) // lane_blk, (col0 + 2 * d_attn) // lane_blk
    nq = seq // tq
    kern = functools.partial(_sba_kernel, tq=tq, scale=1.0 / math.sqrt(ATTN_HEAD_DIM))
    return pl.pallas_call(
        kern,
        grid=(batch, hp, nq),
        in_specs=[pl.BlockSpec((tq, lane_blk), lambda b, h, i: (b * nq + i, qb + h)),
                  pl.BlockSpec((seq, lane_blk), lambda b, h, i: (b, kb + h)),
                  pl.BlockSpec((seq, lane_blk), lambda b, h, i: (b, vb + h))],
        out_specs=pl.BlockSpec((tq, lane_blk), lambda b, h, i: (b * nq + i, h)),
        out_shape=jax.ShapeDtypeStruct((batch * seq, d_attn), F32),
        compiler_params=_params(("parallel", "parallel", "arbitrary")),
    )(proj, proj, proj)


def _ssm_kernel(u_ref, bblk_ref, pw_ref, cblk_ref, d_ref, gw_ref, gb_ref, o_ref, st_ref, x_ref, *, ts, n):
    c = pl.program_id(1)

    @pl.when(c == 0)
    def _():
        st_ref[...] = jnp.zeros_like(st_ref)

    u = u_ref[...]
    x_ref[...] = jnp.dot(u.astype(BF16), bblk_ref[...], preferred_element_type=F32)

    def group(r, carry):
        sr, si = carry
        rows = pl.ds(r * SUBLANES, SUBLANES)
        re = x_ref[rows, 0:n]
        im = x_ref[rows, n:2 * n]
        for lvl, d in enumerate((1, 2, 4)):
            lr = pw_ref[lvl, :, 0:n]
            li = pw_ref[lvl, :, n:2 * n]
            pr = pltpu.roll(re, d, 0)
            pi = pltpu.roll(im, d, 0)
            re, im = re + pr * lr - pi * li, im + pr * li + pi * lr
        lr = pw_ref[3, :, 0:n]
        li = pw_ref[3, :, n:2 * n]
        re, im = re + sr * lr - si * li, im + sr * li + si * lr
        x_ref[rows, 0:n] = re
        x_ref[rows, n:2 * n] = im
        last = slice(SUBLANES - 1, SUBLANES)
        return (jnp.broadcast_to(re[last, :], (SUBLANES, n)), jnp.broadcast_to(im[last, :], (SUBLANES, n)))

    carry = (st_ref[:, 0:n], st_ref[:, n:2 * n])
    for r in range(ts // SUBLANES):
        carry = group(r, carry)
    sr, si = carry
    st_ref[:, 0:n] = sr
    st_ref[:, n:2 * n] = si

    y = jnp.dot(x_ref[...].astype(BF16), cblk_ref[...], preferred_element_type=F32)
    y = _gelu(y + d_ref[...] * u)
    gate = jnp.dot(y.astype(BF16), gw_ref[...], preferred_element_type=F32) + gb_ref[...]
    o_ref[...] = y * _sigmoid(gate)


def _ssm_tables(lam_re, lam_im, log_dt, b_re, b_im, c_re, c_im):
    g, p = lam_re.shape
    hch = b_re.shape[2]
    n = g * p
    lr = jnp.minimum(lam_re.astype(F32), -1e-4)
    li = lam_im.astype(F32)
    dt = jnp.exp(log_dt.astype(F32))[:, None]

    def lam_pow(steps):
        mag = jnp.exp(lr * dt * steps)
        return mag * jnp.cos(li * dt * steps), mag * jnp.sin(li * dt * steps)

    br, bi = lam_pow(1.0)
    norm = lr * lr + li * li
    cr = ((br - 1.0) * lr + bi * li) / norm
    ci = (bi * lr - (br - 1.0) * li) / norm
    bre, bim = b_re.astype(F32), b_im.astype(F32)
    bbar_re = cr[..., None] * bre - ci[..., None] * bim
    bbar_im = cr[..., None] * bim + ci[..., None] * bre
    eye = jnp.eye(g, dtype=F32)
    bb_re = jnp.einsum('gph,gk->ghkp', bbar_re, eye).reshape(g * hch, n)
    bb_im = jnp.einsum('gph,gk->ghkp', bbar_im, eye).reshape(g * hch, n)
    bblk = jnp.concatenate([bb_re, bb_im], axis=1).astype(BF16)
    cc_re = jnp.einsum('ghp,gk->kpgh', c_re.astype(F32), eye).reshape(n, g * hch)
    cc_im = jnp.einsum('ghp,gk->kpgh', c_im.astype(F32), eye).reshape(n, g * hch)
    cblk = jnp.concatenate([cc_re, -cc_im], axis=0).astype(BF16)
    t = jnp.arange(SUBLANES)
    tabs = []
    for d in (1, 2, 4):
        pr, pi = lam_pow(float(d))
        keep = (t >= d)[:, None]
        tabs.append(jnp.concatenate([jnp.where(keep, pr.reshape(1, n), 0.0),
                                     jnp.where(keep, pi.reshape(1, n), 0.0)], axis=1))
    rows = [lam_pow(float(k + 1)) for k in range(SUBLANES)]
    tabs.append(jnp.concatenate([jnp.stack([r[0].reshape(n) for r in rows]),
                                 jnp.stack([r[1].reshape(n) for r in rows])], axis=1))
    return bblk, cblk, jnp.stack(tabs).astype(F32)


def s5_ssm(proj, batch, seq, col0, bblk, cblk, pw, d_skip, glu_w, glu_b, ts):
    ds = bblk.shape[0]
    n = bblk.shape[1] // 2
    nt = seq // ts
    kern = functools.partial(_ssm_kernel, ts=ts, n=n)
    cb = col0 // ds
    return pl.pallas_call(
        kern,
        grid=(batch, nt),
        in_specs=[pl.BlockSpec((ts, ds), lambda b, c: (b * nt + c, cb)),
                  pl.BlockSpec((ds, 2 * n), lambda b, c: (0, 0)),
                  pl.BlockSpec((4, SUBLANES, 2 * n), lambda b, c: (0, 0, 0)),
                  pl.BlockSpec((2 * n, ds), lambda b, c: (0, 0)),
                  pl.BlockSpec((1, ds), lambda b, c: (0, 0)),
                  pl.BlockSpec((ds, ds), lambda b, c: (0, 0)),
                  pl.BlockSpec((1, ds), lambda b, c: (0, 0))],
        out_specs=pl.BlockSpec((ts, ds), lambda b, c: (b * nt + c, 0)),
        out_shape=jax.ShapeDtypeStruct((batch * seq, ds), F32),
        scratch_shapes=[pltpu.VMEM((SUBLANES, 2 * n), F32), pltpu.VMEM((ts, 2 * n), F32)],
        compiler_params=_params(("parallel", "arbitrary")),
    )(proj, bblk, pw, cblk, d_skip.reshape(1, ds), glu_w.astype(BF16), glu_b.reshape(1, ds))


def _mix_mem_kernel(x_ref, yc_ref, ya_ref, ys_ref, gg_ref, wout_ref, g_ref, wq_ref, kv_ref, wo_ref, o_ref, *, heads):
    x = x_ref[...]
    off = 0
    for y_ref in (yc_ref, ya_ref, ys_ref):
        width = y_ref.shape[1]
        h = _rms(y_ref[...], gg_ref[:, off:off + width]).astype(BF16)
        x = x + jnp.dot(h, wout_ref[off:off + width, :], preferred_element_type=F32)
        off += width
    d = x.shape[1]
    dh = d // heads
    h = _rms(x, g_ref[...]).astype(BF16)
    q = jnp.dot(h, wq_ref[...], preferred_element_type=F32) * (1.0 / math.sqrt(dh))
    outs = []
    for hd in range(heads):
        qh = q[:, hd * dh:(hd + 1) * dh].astype(BF16)
        kh = kv_ref[:, hd * dh:(hd + 1) * dh].astype(BF16)
        vh = kv_ref[:, d + hd * dh:d + (hd + 1) * dh].astype(BF16)
        s = lax.dot_general(qh, kh, (((1,), (1,)), ((), ())), preferred_element_type=F32)
        e = jnp.exp(s - jnp.max(s, axis=-1, keepdims=True))
        p = e / jnp.sum(e, axis=-1, keepdims=True)
        outs.append(jnp.dot(p.astype(BF16), vh, preferred_element_type=F32).astype(BF16))
    o = jnp.concatenate(outs, axis=-1)
    o_ref[...] = x + jnp.dot(o, wo_ref[...], preferred_element_type=F32)


def mix_mem(x, yc, ya, ys, g_grp, w_out, g_mem, wq, memkv, wo, seq, n_mem, tm):
    m, d = x.shape
    dc, da, dsm = yc.shape[1], ya.shape[1], ys.shape[1]
    tiles_per_seq = seq // tm
    kern = functools.partial(_mix_mem_kernel, heads=MEM_HEADS)
    row_blk = lambda i: (i, 0)
    whole = lambda i: (0, 0)
    return pl.pallas_call(
        kern,
        grid=(m // tm,),
        in_specs=[pl.BlockSpec((tm, d), row_blk),
                  pl.BlockSpec((tm, dc), row_blk),
                  pl.BlockSpec((tm, da), row_blk),
                  pl.BlockSpec((tm, dsm), row_blk),
                  pl.BlockSpec((1, dc + da + dsm), whole),
                  pl.BlockSpec((dc + da + dsm, d), whole),
                  pl.BlockSpec((1, d), whole),
                  pl.BlockSpec((d, d), whole),
                  pl.BlockSpec((n_mem, 2 * d), lambda i: (i // tiles_per_seq, 0)),
                  pl.BlockSpec((d, d), whole)],
        out_specs=pl.BlockSpec((tm, d), row_blk),
        out_shape=jax.ShapeDtypeStruct((m, d), F32),
        compiler_params=_params(("parallel",)),
    )(x, yc, ya, ys, g_grp.reshape(1, -1), w_out, g_mem.reshape(1, d), wq, memkv, wo)


def _sort16_network():
    def merge(lo, hi, r):
        step = r * 2
        if step < hi - lo:
            yield from merge(lo, hi, step)
            yield from merge(lo + r, hi, step)
            yield from [(i, i + r) for i in range(lo + r, hi - r, step)]
        else:
            yield (lo, lo + r)

    def sort(lo, hi):
        if hi - lo >= 1:
            mid = lo + (hi - lo) // 2
            yield from sort(lo, mid)
            yield from sort(mid + 1, hi)
            yield from merge(lo, hi, 1)

    return tuple(sort(0, PEER_TOPK - 1))


_SORT16 = _sort16_network()


def _exchange(v, i, j):
    v[i], v[j] = jnp.maximum(v[i], v[j]), jnp.minimum(v[i], v[j])


def _top16_sorted(x):
    v = [x[i * SUBLANES:(i + 1) * SUBLANES, :] for i in range(PEER_TOPK)]
    for i, j in _SORT16:
        _exchange(v, i, j)
    for shift in (4, 2, 1):
        v = [jnp.maximum(v[i], pltpu.roll(v[PEER_TOPK - 1 - i], shift, 0)) for i in range(PEER_TOPK)]
        for d in (8, 4, 2, 1):
            for i in range(PEER_TOPK):
                if not i & d:
                    _exchange(v, i, i + d)
    return v


def _peer_route_kernel(x_ref, g_ref, wq_ref, keys_ref, hb_ref, rank2_ref, e2_ref, cnt_ref, c1_ref,
                       q_scr, top_scr, *, tp):
    nk = PEER_KEYS
    h = _rms(x_ref[...], g_ref[...]).astype(BF16)
    hb_ref[...] = h
    q = jnp.dot(h, wq_ref[...], preferred_element_type=F32)
    for c in range(2 * PEER_HEADS):
        q_scr[c] = q[:, c * nk:(c + 1) * nk].astype(BF16)

    def head(hd):
        nt = (((1,), (1,)), ((), ()))
        s1 = lax.dot_general(keys_ref[2 * hd], q_scr[2 * hd], nt, preferred_element_type=F32)
        s2 = lax.dot_general(keys_ref[2 * hd + 1], q_scr[2 * hd + 1], nt, preferred_element_type=F32)
        va = _top16_sorted(s1)
        vb = _top16_sorted(s2)
        for r in range(PEER_TOPK):
            top_scr[0, r:r + 1, :] = va[r][0:1, :]
            top_scr[1, r:r + 1, :] = vb[r][0:1, :]
        a = top_scr[0]
        b = top_scr[1]
        cands = [a[0:1, :] + b]
        for j in range(1, 8):
            cands.append(a[j:j + 1, :] + b[0:8, :])
        cands.append(a[8:16, :] + b[0:1, :])
        cands.append(jnp.full((nk - PAIR_CANDIDATES, tp), -jnp.inf, F32))
        best = _top16_sorted(jnp.concatenate(cands, axis=0))
        thr = best[PEER_TOPK - 1][0:1, :]
        zsum = jnp.zeros_like(best[0])
        for r in range(PEER_TOPK):
            zsum = zsum + jnp.exp(best[r] - best[0])
        groups = (PEER_TOPK, SUBLANES, tp)
        s1g = s1.reshape(groups)
        s2g = s2.reshape(groups)
        cnt = jnp.zeros(groups, F32)
        for j in range(PEER_TOPK):
            cj = jnp.sum(((a[j:j + 1, :] + b) >= thr).astype(F32), axis=0, keepdims=True)
            cnt = jnp.where(s1g == va[j][None], cj[None], cnt)
        rank2 = jnp.full(groups, RANK_NONE, F32)
        for r in reversed(range(PEER_TOPK)):
            rank2 = jnp.where(s2g == vb[r][None], float(r), rank2)
        cnt_ref[hd] = cnt.reshape(nk, tp)
        c1_ref[hd] = (jnp.exp(s1g - va[0][None]) / zsum[None]).reshape(nk, tp)
        packed = (nk // BF16_ROWS, BF16_ROWS, tp)
        rank2_ref[hd] = rank2.reshape(packed).astype(BF16)
        e2_ref[hd] = jnp.exp(s2g - vb[0][None]).reshape(packed).astype(BF16)

    for hd in range(PEER_HEADS):
        head(hd)


def peer_route(x, g, wq, keys, tp):
    m, d = x.shape
    nq = wq.shape[1]
    nk = PEER_KEYS
    kern = functools.partial(_peer_route_kernel, tp=tp)
    tok_blk = lambda i: (0, 0, i)
    packed_blk = lambda i: (0, 0, 0, i)
    packed = (PEER_HEADS, nk // BF16_ROWS, BF16_ROWS)
    return pl.pallas_call(
        kern,
        grid=(m // tp,),
        in_specs=[pl.BlockSpec((tp, d), lambda i: (i, 0)),
                  pl.BlockSpec((1, d), lambda i: (0, 0)),
                  pl.BlockSpec((d, nq), lambda i: (0, 0)),
                  pl.BlockSpec((2 * PEER_HEADS, nk, nk), lambda i: (0, 0, 0))],
        out_specs=[pl.BlockSpec((tp, d), lambda i: (i, 0)),
                   pl.BlockSpec(packed + (tp,), packed_blk),
                   pl.BlockSpec(packed + (tp,), packed_blk),
                   pl.BlockSpec((PEER_HEADS, nk, tp), tok_blk),
                   pl.BlockSpec((PEER_HEADS, nk, tp), tok_blk)],
        out_shape=[jax.ShapeDtypeStruct((m, d), BF16),
                   jax.ShapeDtypeStruct(packed + (m,), BF16),
                   jax.ShapeDtypeStruct(packed + (m,), BF16),
                   jax.ShapeDtypeStruct((PEER_HEADS, nk, m), F32),
                   jax.ShapeDtypeStruct((PEER_HEADS, nk, m), F32)],
        scratch_shapes=[pltpu.VMEM((2 * PEER_HEADS, tp, nk), BF16),
                        pltpu.VMEM((2, PEER_TOPK, tp), F32)],
        compiler_params=_params(("parallel",)),
    )(x, g.reshape(1, d), wq, keys)


def _peer_experts_kernel(x_ref, hb_ref, rank2_ref, e2_ref, cnt_ref, c1_ref, u_ref, vt_ref, fg_ref, o_ref,
                         acc_ref, hw_ref, *, blocks, final_norm):
    e = pl.program_id(1)
    nk = PEER_KEYS
    tt = hb_ref.shape[0]
    tiles = nk // BF16_ROWS

    @pl.when(e == 0)
    def _():
        acc_ref[...] = jnp.zeros_like(acc_ref)

    pre = lax.dot_general(u_ref[...], hb_ref[...], (((1,), (1,)), ((), ())), preferred_element_type=F32)
    for j in range(blocks):
        w = None
        for hd in range(PEER_HEADS):
            cnt = jnp.broadcast_to(cnt_ref[hd, j:j + 1, :], (BF16_ROWS, tt)).astype(BF16)
            c1 = jnp.broadcast_to(c1_ref[hd, j:j + 1, :], (BF16_ROWS, tt)).astype(BF16)
            term = jnp.where(rank2_ref[hd] < cnt[None], e2_ref[hd], jnp.zeros((), BF16)) * c1[None]
            w = term if w is None else w + term
        act = _gelu(pre[j * nk:(j + 1) * nk, :].reshape(tiles, BF16_ROWS, tt).astype(BF16))
        hw_ref[j * tiles:(j + 1) * tiles] = act * w
    hw = hw_ref[...].reshape(blocks * nk, tt)
    acc_ref[...] += jnp.dot(vt_ref[...], hw, preferred_element_type=F32)

    @pl.when(e == pl.num_programs(1) - 1)
    def _():
        out = x_ref[...] + acc_ref[...].T
        o_ref[...] = _rms(out, fg_ref[...]) if final_norm else out


def peer_experts(x, hb, rank2, e2, cnt, c1, u_tabs, vt_tabs, layer, final_g, final_norm, tt, blocks):
    m, d = x.shape
    n_exp = u_tabs.shape[1]
    nk = PEER_KEYS
    te = blocks * nk
    kern = functools.partial(_peer_experts_kernel, blocks=blocks, final_norm=final_norm)
    packed = (PEER_HEADS, nk // BF16_ROWS, BF16_ROWS, tt)
    return pl.pallas_call(
        kern,
        grid=(m // tt, n_exp // te),
        in_specs=[pl.BlockSpec((tt, d), lambda t, e: (t, 0)),
                  pl.BlockSpec((tt, d), lambda t, e: (t, 0)),
                  pl.BlockSpec(packed, lambda t, e: (0, 0, 0, t)),
                  pl.BlockSpec(packed, lambda t, e: (0, 0, 0, t)),
                  pl.BlockSpec((PEER_HEADS, blocks, tt), lambda t, e: (0, e, t)),
                  pl.BlockSpec((PEER_HEADS, blocks, tt), lambda t, e: (0, e, t)),
                  pl.BlockSpec((None, te, d), lambda t, e: (layer, e, 0)),
                  pl.BlockSpec((None, d, te), lambda t, e: (layer, 0, e)),
                  pl.BlockSpec((1, d), lambda t, e: (0, 0))],
        out_specs=pl.BlockSpec((tt, d), lambda t, e: (t, 0)),
        out_shape=jax.ShapeDtypeStruct((m, d), F32),
        scratch_shapes=[pltpu.VMEM((d, tt), F32), pltpu.VMEM((te // BF16_ROWS, BF16_ROWS, tt), BF16)],
        compiler_params=_params(("parallel", "arbitrary")),
    )(x, hb, rank2, e2, cnt, c1, u_tabs, vt_tabs, final_g.reshape(1, d))


def _tiles(seq, n_mem):
    return dict(
        proj=min(512, seq), conv=min(512, seq), sba=min(256, seq), ssm=min(512, seq), mix=min(512, seq),
        memkv=min(256, n_mem), route=min(512, seq), experts=min(1024, seq), blocks=8)


def kernel(x, mem, norm_mix_g, w_in, conv_w, conv_b, conv_ln_g, conv_ln_b, ssm_lambda_re, ssm_lambda_im, ssm_log_dt, ssm_b_re, ssm_b_im, ssm_c_re, ssm_c_im, ssm_d, ssm_glu_w, ssm_glu_b, grp_norm_g, w_out, norm_mem_g, mem_norm_g, w_mq, w_mkv, w_mo, norm_ffn_g, peer_wq, peer_sub_keys, peer_u, peer_v, final_norm_g):
    batch, seq, d = x.shape
    n_mem = mem.shape[1]
    depth = w_in.shape[0]
    d_conv = conv_w.shape[2]
    d_ssm = ssm_d.shape[1]
    d_attn = (w_in.shape[2] - 2 * d_conv - d_ssm) // 3
    t = _tiles(seq, n_mem)

    xf = x.reshape(batch * seq, d)
    memf = mem.reshape(batch * n_mem, d)
    u_tabs = peer_u.astype(BF16)
    vt_tabs = jnp.swapaxes(peer_v, 1, 2).astype(BF16)
    for l in range(depth):
        proj = norm_matmul(xf, norm_mix_g[l], w_in[l].astype(BF16), t['proj'])
        yc = conformer_conv(proj, conv_w[l], conv_b[l], conv_ln_g[l], conv_ln_b[l], seq, t['conv'])
        ya = stick_breaking_attention(proj, batch, seq, 2 * d_conv, d_attn, t['sba'])
        bblk, cblk, pw = _ssm_tables(ssm_lambda_re[l], ssm_lambda_im[l], ssm_log_dt[l], ssm_b_re[l], ssm_b_im[l],
                                     ssm_c_re[l], ssm_c_im[l])
        ys = s5_ssm(proj, batch, seq, 2 * d_conv + 3 * d_attn, bblk, cblk, pw, ssm_d[l], ssm_glu_w[l], ssm_glu_b[l],
                    t['ssm'])
        memkv = norm_matmul(memf, mem_norm_g[l], w_mkv[l].astype(BF16), t['memkv'])
        xf = mix_mem(xf, yc, ya, ys, grp_norm_g[l], w_out[l].astype(BF16), norm_mem_g[l], w_mq[l].astype(BF16), memkv,
                     w_mo[l].astype(BF16), seq, n_mem, t['mix'])

        keys = peer_sub_keys[l].reshape(2 * PEER_HEADS, PEER_KEYS, -1).astype(BF16)
        hb, rank2, e2, cnt, c1 = peer_route(xf, norm_ffn_g[l], peer_wq[l].astype(BF16), keys, t['route'])
        xf = peer_experts(xf, hb, rank2, e2, cnt, c1, u_tabs, vt_tabs, l, final_norm_g, l == depth - 1,
                          t['experts'], t['blocks'])
    return xf.reshape(batch, seq, d)
```

```python
import functools
import math

import jax
import jax.numpy as jnp
from jax import lax
from jax.experimental import pallas as pl
from jax.experimental.pallas import tpu as pltpu

F32 = jnp.float32
BF16 = jnp.bfloat16
EPS = 1e-6

CONV_WIDTH = 31
SUBLANES = 8
CONV_HALO = 32
ATTN_HEAD_DIM = 64
SBA_HEADS_PER_STEP = 4
LOG2E = 1.4426950408889634
UNDERFLOW_LOG2 = -160.0
SSM_GROUP = 16
SSM_STATE = 64
MEM_HEADS = 4
PEER_HEADS = 8
PEER_KEYS = 128
PEER_TOPK = 16
PAIR_CANDIDATES = 80
RANK_NONE = 99.0
BF16_ROWS = 16
VMEM_LIMIT = 50 * 1024 * 1024


def _params(sem, vmem=VMEM_LIMIT):
    return pltpu.CompilerParams(dimension_semantics=sem, vmem_limit_bytes=vmem)


def _rms(x, g):
    ms = jnp.mean(x * x, axis=-1, keepdims=True)
    return x * lax.rsqrt(ms + EPS) * g


def _gelu(x):
    c = math.sqrt(2.0 / math.pi)
    return 0.5 * x * (1.0 + jnp.tanh(c * (x + 0.044715 * (x * x * x))))


def _sigmoid(x):
    return 1.0 / (1.0 + jnp.exp(-x))


def _norm_matmul_kernel(x_ref, g_ref, w_ref, o_ref):
    h = _rms(x_ref[...], g_ref[...]).astype(BF16)
    o_ref[...] = jnp.dot(h, w_ref[...], preferred_element_type=F32)


def norm_matmul(x, g, w, tm):
    m, d = x.shape
    n = w.shape[1]
    return pl.pallas_call(
        _norm_matmul_kernel,
        grid=(m // tm,),
        in_specs=[pl.BlockSpec((tm, d), lambda i: (i, 0)),
                  pl.BlockSpec((1, d), lambda i: (0, 0)),
                  pl.BlockSpec((d, n), lambda i: (0, 0))],
        out_specs=pl.BlockSpec((tm, n), lambda i: (i, 0)),
        out_shape=jax.ShapeDtypeStruct((m, n), F32),
        compiler_params=_params(("parallel",)),
    )(x, g.reshape(1, d), w)


def _conv_kernel(cur_ref, halo_ref, w_ref, b_ref, lng_ref, lnb_ref, o_ref, ext_ref, phase_ref, *, ts, tiles_per_seq,
                 dc):
    i = pl.program_id(0)
    cur = cur_ref[...]
    halo = halo_ref[...]
    y = cur[:, :dc] * _sigmoid(cur[:, dc:])
    yh = halo[:, :dc] * _sigmoid(halo[:, dc:])
    yh = jnp.where(i % tiles_per_seq == 0, 0.0, yh)
    ext_ref[0:CONV_HALO, :] = yh
    ext_ref[CONV_HALO:, :] = y
    span = ts + CONV_HALO - SUBLANES
    for r in range(1, SUBLANES):
        phase_ref[r - 1, 0:span, :] = ext_ref[r:r + span, :]
    acc = jnp.broadcast_to(b_ref[...], (ts, dc))
    for k in range(CONV_WIDTH):
        off = CONV_HALO - (CONV_WIDTH - 1) + k
        phase, base = off % SUBLANES, off - off % SUBLANES
        window = ext_ref[base:base + ts, :] if phase == 0 else phase_ref[phase - 1, base:base + ts, :]
        acc = acc + w_ref[k:k + 1, :] * window
    mu = jnp.mean(acc, axis=-1, keepdims=True)
    cen = acc - mu
    var = jnp.mean(cen * cen, axis=-1, keepdims=True)
    z = cen * lax.rsqrt(var + EPS) * lng_ref[...] + lnb_ref[...]
    o_ref[...] = z * _sigmoid(z)


def conformer_conv(proj, conv_w, conv_b, ln_g, ln_b, seq, ts):
    m = proj.shape[0]
    dc = conv_w.shape[1]
    kern = functools.partial(_conv_kernel, ts=ts, tiles_per_seq=seq // ts, dc=dc)
    halo_blocks = ts // CONV_HALO
    return pl.pallas_call(
        kern,
        grid=(m // ts,),
        in_specs=[pl.BlockSpec((ts, 2 * dc), lambda i: (i, 0)),
                  pl.BlockSpec((CONV_HALO, 2 * dc), lambda i: (jnp.maximum(i * halo_blocks - 1, 0), 0)),
                  pl.BlockSpec((CONV_WIDTH, dc), lambda i: (0, 0)),
                  pl.BlockSpec((1, dc), lambda i: (0, 0)),
                  pl.BlockSpec((1, dc), lambda i: (0, 0)),
                  pl.BlockSpec((1, dc), lambda i: (0, 0))],
        out_specs=pl.BlockSpec((ts, dc), lambda i: (i, 0)),
        out_shape=jax.ShapeDtypeStruct((m, dc), F32),
        scratch_shapes=[pltpu.VMEM((ts + CONV_HALO, dc), F32),
                        pltpu.VMEM((SUBLANES - 1, ts + CONV_HALO - SUBLANES, dc), F32)],
        compiler_params=_params(("parallel",)),
    )(proj, proj, conv_w, conv_b.reshape(1, dc), ln_g.reshape(1, dc), ln_b.reshape(1, dc))


def _sba_kernel(q_ref, k_ref, v_ref, o_ref, *, tq, scale):
    i = pl.program_id(2)
    dh = ATTN_HEAD_DIM
    row = lax.broadcasted_iota(jnp.int32, (tq, tq), 0)
    col = lax.broadcasted_iota(jnp.int32, (tq, tq), 1)
    causal = col < row
    suffix = (row > col).astype(BF16)

    heads = q_ref.shape[1] // dh
    lanes = [slice(hh * dh, (hh + 1) * dh) for hh in range(heads)]
    qs = [(q_ref[:, ln] * (scale * LOG2E)).astype(BF16) for ln in lanes]

    def block(j, carry, diagonal):
        start = pl.multiple_of(j * tq, tq)
        hs = range(heads)
        kj = [k_ref[pl.ds(start, tq), lanes[h]].astype(BF16) for h in hs]
        vj = [v_ref[pl.ds(start, tq), lanes[h]].astype(BF16) for h in hs]
        z = [lax.dot_general(qs[h], kj[h], (((1,), (1,)), ((), ())), preferred_element_type=F32) for h in hs]
        log_beta, log_stay, later = [], [], []
        for h in hs:
            sp = jnp.log2(1.0 + jnp.exp2(-jnp.abs(z[h])))
            log_beta.append(jnp.minimum(z[h], 0.0) - sp)
            stay = log_beta[h] - z[h]
            log_stay.append(jnp.where(causal, stay, 0.0) if diagonal else stay)
        for h in hs:
            later.append(jnp.dot(log_stay[h].astype(BF16), suffix, preferred_element_type=F32))
        out = []
        for h in hs:
            acc, run = carry[h]
            a = jnp.exp2(log_beta[h] + later[h] + run)
            if diagonal:
                a = jnp.where(causal, a, 0.0)
            acc = acc + jnp.dot(a.astype(BF16), vj[h], preferred_element_type=F32)
            run = run + later[h][:, 0:1] + log_stay[h][:, 0:1]
            out.append((acc, run))
        return tuple(out)

    def unfinished(c):
        return jnp.logical_and(c[0] < i, c[1])

    def sweep(c):
        jj, _, carry = c
        carry = block(i - 1 - jj, carry, False)
        top = carry[0][1]
        for h in range(1, heads):
            top = jnp.maximum(top, carry[h][1])
        return jj + 1, jnp.max(top) > UNDERFLOW_LOG2, carry

    carry = tuple((jnp.zeros((tq, dh), F32), jnp.zeros((tq, 1), F32)) for _ in range(heads))
    carry = block(i, carry, True)
    gated = tuple((acc, jnp.where(i > 0, run, -jnp.inf)) for acc, run in carry)
    carry = block(jnp.maximum(i - 1, 0), gated, False)
    top = carry[0][1]
    for h in range(1, heads):
        top = jnp.maximum(top, carry[h][1])
    _, _, carry = lax.while_loop(unfinished, sweep, (jnp.int32(1), jnp.max(top) > UNDERFLOW_LOG2, carry))
    for hh in range(heads):
        o_ref[:, lanes[hh]] = carry[hh][0]


def stick_breaking_attention(proj, batch, seq, col0, d_attn, tq):
    lane_blk = SBA_HEADS_PER_STEP * ATTN_HEAD_DIM
    hp = d_attn // lane_blk
    qb, kb, vb = col0 // lane_blk, (col0 + d_attn) // lane_blk, (col0 + 2 * d_attn) // lane_blk
    nq = seq // tq
    kern = functools.partial(_sba_kernel, tq=tq, scale=1.0 / math.sqrt(ATTN_HEAD_DIM))
    return pl.pallas_call(
        kern,
        grid=(batch, hp, nq),
        in_specs=[pl.BlockSpec((tq, lane_blk), lambda b, h, i: (b * nq + i, qb + h)),
                  pl.BlockSpec((seq, lane_blk), lambda b, h, i: (b, kb + h)),
                  pl.BlockSpec((seq, lane_blk), lambda b, h, i: (b, vb + h))],
        out_specs=pl.BlockSpec((tq, lane_blk), lambda b, h, i: (b * nq + i, h)),
        out_shape=jax.ShapeDtypeStruct((batch * seq, d_attn), F32),
        compiler_params=_params(("parallel", "parallel", "arbitrary")),
    )(proj, proj, proj)


def _ssm_kernel(u_ref, bblk_ref, pw_ref, cblk_ref, d_ref, gw_ref, gb_ref, o_ref, st_ref, x_ref, *, ts, n):
    c = pl.program_id(1)

    @pl.when(c == 0)
    def _():
        st_ref[...] = jnp.zeros_like(st_ref)

    u = u_ref[...]
    x_ref[...] = jnp.dot(u.astype(BF16), bblk_ref[...], preferred_element_type=F32)

    def group(r, carry):
        sr, si = carry
        rows = pl.ds(r * SUBLANES, SUBLANES)
        re = x_ref[rows, 0:n]
        im = x_ref[rows, n:2 * n]
        for lvl, d in enumerate((1, 2, 4)):
            lr = pw_ref[lvl, :, 0:n]
            li = pw_ref[lvl, :, n:2 * n]
            pr = pltpu.roll(re, d, 0)
            pi = pltpu.roll(im, d, 0)
            re, im = re + pr * lr - pi * li, im + pr * li + pi * lr
        lr = pw_ref[3, :, 0:n]
        li = pw_ref[3, :, n:2 * n]
        re, im = re + sr * lr - si * li, im + sr * li + si * lr
        x_ref[rows, 0:n] = re
        x_ref[rows, n:2 * n] = im
        last = slice(SUBLANES - 1, SUBLANES)
        return (jnp.broadcast_to(re[last, :], (SUBLANES, n)), jnp.broadcast_to(im[last, :], (SUBLANES, n)))

    carry = (st_ref[:, 0:n], st_ref[:, n:2 * n])
    for r in range(ts // SUBLANES):
        carry = group(r, carry)
    sr, si = carry
    st_ref[:, 0:n] = sr
    st_ref[:, n:2 * n] = si

    y = jnp.dot(x_ref[...].astype(BF16), cblk_ref[...], preferred_element_type=F32)
    y = _gelu(y + d_ref[...] * u)
    gate = jnp.dot(y.astype(BF16), gw_ref[...], preferred_element_type=F32) + gb_ref[...]
    o_ref[...] = y * _sigmoid(gate)


def _ssm_tables(lam_re, lam_im, log_dt, b_re, b_im, c_re, c_im):
    g, p = lam_re.shape
    hch = b_re.shape[2]
    n = g * p
    lr = jnp.minimum(lam_re.astype(F32), -1e-4)
    li = lam_im.astype(F32)
    dt = jnp.exp(log_dt.astype(F32))[:, None]

    def lam_pow(steps):
        mag = jnp.exp(lr * dt * steps)
        return mag * jnp.cos(li * dt * steps), mag * jnp.sin(li * dt * steps)

    br, bi = lam_pow(1.0)
    norm = lr * lr + li * li
    cr = ((br - 1.0) * lr + bi * li) / norm
    ci = (bi * lr - (br - 1.0) * li) / norm
    bre, bim = b_re.astype(F32), b_im.astype(F32)
    bbar_re = cr[..., None] * bre - ci[..., None] * bim
    bbar_im = cr[..., None] * bim + ci[..., None] * bre
    eye = jnp.eye(g, dtype=F32)
    bb_re = jnp.einsum('gph,gk->ghkp', bbar_re, eye).reshape(g * hch, n)
    bb_im = jnp.einsum('gph,gk->ghkp', bbar_im, eye).reshape(g * hch, n)
    bblk = jnp.concatenate([bb_re, bb_im], axis=1).astype(BF16)
    cc_re = jnp.einsum('ghp,gk->kpgh', c_re.astype(F32), eye).reshape(n, g * hch)
    cc_im = jnp.einsum('ghp,gk->kpgh', c_im.astype(F32), eye).reshape(n, g * hch)
    cblk = jnp.concatenate([cc_re, -cc_im], axis=0).astype(BF16)
    t = jnp.arange(SUBLANES)
    tabs = []
    for d in (1, 2, 4):
        pr, pi = lam_pow(float(d))
        keep = (t >= d)[:, None]
        tabs.append(jnp.concatenate([jnp.where(keep, pr.reshape(1, n), 0.0),
                                     jnp.where(keep, pi.reshape(1, n), 0.0)], axis=1))
    rows = [lam_pow(float(k + 1)) for k in range(SUBLANES)]
    tabs.append(jnp.concatenate([jnp.stack([r[0].reshape(n) for r in rows]),
                                 jnp.stack([r[1].reshape(n) for r in rows])], axis=1))
    return bblk, cblk, jnp.stack(tabs).astype(F32)


def s5_ssm(proj, batch, seq, col0, bblk, cblk, pw, d_skip, glu_w, glu_b, ts):
    ds = bblk.shape[0]
    n = bblk.shape[1] // 2
    nt = seq // ts
    kern = functools.partial(_ssm_kernel, ts=ts, n=n)
    cb = col0 // ds
    return pl.pallas_call(
        kern,
        grid=(batch, nt),
        in_specs=[pl.BlockSpec((ts, ds), lambda b, c: (b * nt + c, cb)),
                  pl.BlockSpec((ds, 2 * n), lambda b, c: (0, 0)),
                  pl.BlockSpec((4, SUBLANES, 2 * n), lambda b, c: (0, 0, 0)),
                  pl.BlockSpec((2 * n, ds), lambda b, c: (0, 0)),
                  pl.BlockSpec((1, ds), lambda b, c: (0, 0)),
                  pl.BlockSpec((ds, ds), lambda b, c: (0, 0)),
                  pl.BlockSpec((1, ds), lambda b, c: (0, 0))],
        out_specs=pl.BlockSpec((ts, ds), lambda b, c: (b * nt + c, 0)),
        out_shape=jax.ShapeDtypeStruct((batch * seq, ds), F32),
        scratch_shapes=[pltpu.VMEM((SUBLANES, 2 * n), F32), pltpu.VMEM((ts, 2 * n), F32)],
        compiler_params=_params(("parallel", "arbitrary")),
    )(proj, bblk, pw, cblk, d_skip.reshape(1, ds), glu_w.astype(BF16), glu_b.reshape(1, ds))


def _mix_mem_kernel(x_ref, yc_ref, ya_ref, ys_ref, gg_ref, wout_ref, g_ref, wq_ref, kv_ref, wo_ref, o_ref, *, heads):
    x = x_ref[...]
    off = 0
    for y_ref in (yc_ref, ya_ref, ys_ref):
        width = y_ref.shape[1]
        h = _rms(y_ref[...], gg_ref[:, off:off + width]).astype(BF16)
        x = x + jnp.dot(h, wout_ref[off:off + width, :], preferred_element_type=F32)
        off += width
    d = x.shape[1]
    dh = d // heads
    h = _rms(x, g_ref[...]).astype(BF16)
    q = jnp.dot(h, wq_ref[...], preferred_element_type=F32) * (1.0 / math.sqrt(dh))
    outs = []
    for hd in range(heads):
        qh = q[:, hd * dh:(hd + 1) * dh].astype(BF16)
        kh = kv_ref[:, hd * dh:(hd + 1) * dh].astype(BF16)
        vh = kv_ref[:, d + hd * dh:d + (hd + 1) * dh].astype(BF16)
        s = lax.dot_general(qh, kh, (((1,), (1,)), ((), ())), preferred_element_type=F32)
        e = jnp.exp(s - jnp.max(s, axis=-1, keepdims=True))
        p = e / jnp.sum(e, axis=-1, keepdims=True)
        outs.append(jnp.dot(p.astype(BF16), vh, preferred_element_type=F32).astype(BF16))
    o = jnp.concatenate(outs, axis=-1)
    o_ref[...] = x + jnp.dot(o, wo_ref[...], preferred_element_type=F32)


def mix_mem(x, yc, ya, ys, g_grp, w_out, g_mem, wq, memkv, wo, seq, n_mem, tm):
    m, d = x.shape
    dc, da, dsm = yc.shape[1], ya.shape[1], ys.shape[1]
    tiles_per_seq = seq // tm
    kern = functools.partial(_mix_mem_kernel, heads=MEM_HEADS)
    row_blk = lambda i: (i, 0)
    whole = lambda i: (0, 0)
    return pl.pallas_call(
        kern,
        grid=(m // tm,),
        in_specs=[pl.BlockSpec((tm, d), row_blk),
                  pl.BlockSpec((tm, dc), row_blk),
                  pl.BlockSpec((tm, da), row_blk),
                  pl.BlockSpec((tm, dsm), row_blk),
                  pl.BlockSpec((1, dc + da + dsm), whole),
                  pl.BlockSpec((dc + da + dsm, d), whole),
                  pl.BlockSpec((1, d), whole),
                  pl.BlockSpec((d, d), whole),
                  pl.BlockSpec((n_mem, 2 * d), lambda i: (i // tiles_per_seq, 0)),
                  pl.BlockSpec((d, d), whole)],
        out_specs=pl.BlockSpec((tm, d), row_blk),
        out_shape=jax.ShapeDtypeStruct((m, d), F32),
        compiler_params=_params(("parallel",)),
    )(x, yc, ya, ys, g_grp.reshape(1, -1), w_out, g_mem.reshape(1, d), wq, memkv, wo)


def _sort16_network():
    def merge(lo, hi, r):
        step = r * 2
        if step < hi - lo:
            yield from merge(lo, hi, step)
            yield from merge(lo + r, hi, step)
            yield from [(i, i + r) for i in range(lo + r, hi - r, step)]
        else:
            yield (lo, lo + r)

    def sort(lo, hi):
        if hi - lo >= 1:
            mid = lo + (hi - lo) // 2
            yield from sort(lo, mid)
            yield from sort(mid + 1, hi)
            yield from merge(lo, hi, 1)

    return tuple(sort(0, PEER_TOPK - 1))


_SORT16 = _sort16_network()


def _exchange(v, i, j):
    v[i], v[j] = jnp.maximum(v[i], v[j]), jnp.minimum(v[i], v[j])


def _top16_sorted(x):
    v = [x[i * SUBLANES:(i + 1) * SUBLANES, :] for i in range(PEER_TOPK)]
    for i, j in _SORT16:
        _exchange(v, i, j)
    for shift in (4, 2, 1):
        v = [jnp.maximum(v[i], pltpu.roll(v[PEER_TOPK - 1 - i], shift, 0)) for i in range(PEER_TOPK)]
        for d in (8, 4, 2, 1):
            for i in range(PEER_TOPK):
                if not i & d:
                    _exchange(v, i, i + d)
    return v


def _peer_route_kernel(x_ref, g_ref, wq_ref, keys_ref, hb_ref, rank2_ref, e2_ref, cnt_ref, c1_ref,
                       q_scr, top_scr, *, tp):
    nk = PEER_KEYS
    h = _rms(x_ref[...], g_ref[...]).astype(BF16)
    hb_ref[...] = h
    q = jnp.dot(h, wq_ref[...], preferred_element_type=F32)
    for c in range(2 * PEER_HEADS):
        q_scr[c] = q[:, c * nk:(c + 1) * nk].astype(BF16)

    def head(hd):
        nt = (((1,), (1,)), ((), ()))
        s1 = lax.dot_general(keys_ref[2 * hd], q_scr[2 * hd], nt, preferred_element_type=F32)
        s2 = lax.dot_general(keys_ref[2 * hd + 1], q_scr[2 * hd + 1], nt, preferred_element_type=F32)
        va = _top16_sorted(s1)
        vb = _top16_sorted(s2)
        for r in range(PEER_TOPK):
            top_scr[0, r:r + 1, :] = va[r][0:1, :]
            top_scr[1, r:r + 1, :] = vb[r][0:1, :]
        a = top_scr[0]
        b = top_scr[1]
        cands = [a[0:1, :] + b]
        for j in range(1, 8):
            cands.append(a[j:j + 1, :] + b[0:8, :])
        cands.append(a[8:16, :] + b[0:1, :])
        cands.append(jnp.full((nk - PAIR_CANDIDATES, tp), -jnp.inf, F32))
        best = _top16_sorted(jnp.concatenate(cands, axis=0))
        thr = best[PEER_TOPK - 1][0:1, :]
        zsum = jnp.zeros_like(best[0])
        for r in range(PEER_TOPK):
            zsum = zsum + jnp.exp(best[r] - best[0])
        groups = (PEER_TOPK, SUBLANES, tp)
        s1g = s1.reshape(groups)
        s2g = s2.reshape(groups)
        cnt = jnp.zeros(groups, F32)
        for j in range(PEER_TOPK):
            cj = jnp.sum(((a[j:j + 1, :] + b) >= thr).astype(F32), axis=0, keepdims=True)
            cnt = jnp.where(s1g == va[j][None], cj[None], cnt)
        rank2 = jnp.full(groups, RANK_NONE, F32)
        for r in reversed(range(PEER_TOPK)):
            rank2 = jnp.where(s2g == vb[r][None], float(r), rank2)
        cnt_ref[hd] = cnt.reshape(nk, tp)
        c1_ref[hd] = (jnp.exp(s1g - va[0][None]) / zsum[None]).reshape(nk, tp)
        packed = (nk // BF16_ROWS, BF16_ROWS, tp)
        rank2_ref[hd] = rank2.reshape(packed).astype(BF16)
        e2_ref[hd] = jnp.exp(s2g - vb[0][None]).reshape(packed).astype(BF16)

    for hd in range(PEER_HEADS):
        head(hd)


def peer_route(x, g, wq, keys, tp):
    m, d = x.shape
    nq = wq.shape[1]
    nk = PEER_KEYS
    kern = functools.partial(_peer_route_kernel, tp=tp)
    tok_blk = lambda i: (0, 0, i)
    packed_blk = lambda i: (0, 0, 0, i)
    packed = (PEER_HEADS, nk // BF16_ROWS, BF16_ROWS)
    return pl.pallas_call(
        kern,
        grid=(m // tp,),
        in_specs=[pl.BlockSpec((tp, d), lambda i: (i, 0)),
                  pl.BlockSpec((1, d), lambda i: (0, 0)),
                  pl.BlockSpec((d, nq), lambda i: (0, 0)),
                  pl.BlockSpec((2 * PEER_HEADS, nk, nk), lambda i: (0, 0, 0))],
        out_specs=[pl.BlockSpec((tp, d), lambda i: (i, 0)),
                   pl.BlockSpec(packed + (tp,), packed_blk),
                   pl.BlockSpec(packed + (tp,), packed_blk),
                   pl.BlockSpec((PEER_HEADS, nk, tp), tok_blk),
                   pl.BlockSpec((PEER_HEADS, nk, tp), tok_blk)],
        out_shape=[jax.ShapeDtypeStruct((m, d), BF16),
                   jax.ShapeDtypeStruct(packed + (m,), BF16),
                   jax.ShapeDtypeStruct(packed + (m,), BF16),
                   jax.ShapeDtypeStruct((PEER_HEADS, nk, m), F32),
                   jax.ShapeDtypeStruct((PEER_HEADS, nk, m), F32)],
        scratch_shapes=[pltpu.VMEM((2 * PEER_HEADS, tp, nk), BF16),
                        pltpu.VMEM((2, PEER_TOPK, tp), F32)],
        compiler_params=_params(("parallel",)),
    )(x, g.reshape(1, d), wq, keys)


def _peer_experts_kernel(x_ref, hb_ref, rank2_ref, e2_ref, cnt_ref, c1_ref, u_ref, vt_ref, fg_ref, o_ref,
                         acc_ref, hw_ref, *, blocks, final_norm):
    e = pl.program_id(1)
    nk = PEER_KEYS
    tt = hb_ref.shape[0]
    tiles = nk // BF16_ROWS

    @pl.when(e == 0)
    def _():
        acc_ref[...] = jnp.zeros_like(acc_ref)

    pre = lax.dot_general(u_ref[...], hb_ref[...], (((1,), (1,)), ((), ())), preferred_element_type=F32)
    for j in range(blocks):
        w = None
        for hd in range(PEER_HEADS):
            cnt = jnp.broadcast_to(cnt_ref[hd, j:j + 1, :], (BF16_ROWS, tt)).astype(BF16)
            c1 = jnp.broadcast_to(c1_ref[hd, j:j + 1, :], (BF16_ROWS, tt)).astype(BF16)
            term = jnp.where(rank2_ref[hd] < cnt[None], e2_ref[hd], jnp.zeros((), BF16)) * c1[None]
            w = term if w is None else w + term
        act = _gelu(pre[j * nk:(j + 1) * nk, :].reshape(tiles, BF16_ROWS, tt).astype(BF16))
        hw_ref[j * tiles:(j + 1) * tiles] = act * w
    hw = hw_ref[...].reshape(blocks * nk, tt)
    acc_ref[...] += jnp.dot(vt_ref[...], hw, preferred_element_type=F32)

    @pl.when(e == pl.num_programs(1) - 1)
    def _():
        out = x_ref[...] + acc_ref[...].T
        o_ref[...] = _rms(out, fg_ref[...]) if final_norm else out


def peer_experts(x, hb, rank2, e2, cnt, c1, u_tabs, vt_tabs, layer, final_g, final_norm, tt, blocks):
    m, d = x.shape
    n_exp = u_tabs.shape[1]
    nk = PEER_KEYS
    te = blocks * nk
    kern = functools.partial(_peer_experts_kernel, blocks=blocks, final_norm=final_norm)
    packed = (PEER_HEADS, nk // BF16_ROWS, BF16_ROWS, tt)
    return pl.pallas_call(
        kern,
        grid=(m // tt, n_exp // te),
        in_specs=[pl.BlockSpec((tt, d), lambda t, e: (t, 0)),
                  pl.BlockSpec((tt, d), lambda t, e: (t, 0)),
                  pl.BlockSpec(packed, lambda t, e: (0, 0, 0, t)),
                  pl.BlockSpec(packed, lambda t, e: (0, 0, 0, t)),
                  pl.BlockSpec((PEER_HEADS, blocks, tt), lambda t, e: (0, e, t)),
                  pl.BlockSpec((PEER_HEADS, blocks, tt), lambda t, e: (0, e, t)),
                  pl.BlockSpec((None, te, d), lambda t, e: (layer, e, 0)),
                  pl.BlockSpec((None, d, te), lambda t, e: (layer, 0, e)),
                  pl.BlockSpec((1, d), lambda t, e: (0, 0))],
        out_specs=pl.BlockSpec((tt, d), lambda t, e: (t, 0)),
        out_shape=jax.ShapeDtypeStruct((m, d), F32),
        scratch_shapes=[pltpu.VMEM((d, tt), F32), pltpu.VMEM((te // BF16_ROWS, BF16_ROWS, tt), BF16)],
        compiler_params=_params(("parallel", "arbitrary")),
    )(x, hb, rank2, e2, cnt, c1, u_tabs, vt_tabs, final_g.reshape(1, d))


def _tiles(seq, n_mem):
    return dict(
        proj=min(512, seq), conv=min(512, seq), sba=min(256, seq), ssm=min(512, seq), mix=min(512, seq),
        memkv=min(256, n_mem), route=min(512, seq), experts=min(1024, seq), blocks=8)


def kernel(x, mem, norm_mix_g, w_in, conv_w, conv_b, conv_ln_g, conv_ln_b, ssm_lambda_re, ssm_lambda_im, ssm_log_dt, ssm_b_re, ssm_b_im, ssm_c_re, ssm_c_im, ssm_d, ssm_glu_w, ssm_glu_b, grp_norm_g, w_out, norm_mem_g, mem_norm_g, w_mq, w_mkv, w_mo, norm_ffn_g, peer_wq, peer_sub_keys, peer_u, peer_v, final_norm_g):
    batch, seq, d = x.shape
    n_mem = mem.shape[1]
    depth = w_in.shape[0]
    d_conv = conv_w.shape[2]
    d_ssm = ssm_d.shape[1]
    d_attn = (w_in.shape[2] - 2 * d_conv - d_ssm) // 3
    t = _tiles(seq, n_mem)
    assert all(seq % t[k] == 0 for k in ('proj', 'conv', 'sba', 'ssm', 'mix', 'route', 'experts')), (seq, t)
    assert n_mem % t['memkv'] == 0 and t['conv'] % CONV_HALO == 0 and CONV_HALO >= CONV_WIDTH - 1
    assert d_attn % (SBA_HEADS_PER_STEP * ATTN_HEAD_DIM) == 0 and d % MEM_HEADS == 0
    assert peer_u.shape[1] == PEER_KEYS * PEER_KEYS and peer_u.shape[1] % (t['blocks'] * PEER_KEYS) == 0
    assert peer_sub_keys.shape[1:] == (PEER_HEADS, 2, PEER_KEYS, PEER_KEYS)
    assert peer_wq.shape[2] == 2 * PEER_HEADS * PEER_KEYS

    xf = x.reshape(batch * seq, d)
    memf = mem.reshape(batch * n_mem, d)
    u_tabs = peer_u.astype(BF16)
    vt_tabs = jnp.swapaxes(peer_v, 1, 2).astype(BF16)
    for l in range(depth):
        proj = norm_matmul(xf, norm_mix_g[l], w_in[l].astype(BF16), t['proj'])
        yc = conformer_conv(proj, conv_w[l], conv_b[l], conv_ln_g[l], conv_ln_b[l], seq, t['conv'])
        ya = stick_breaking_attention(proj, batch, seq, 2 * d_conv, d_attn, t['sba'])
        bblk, cblk, pw = _ssm_tables(ssm_lambda_re[l], ssm_lambda_im[l], ssm_log_dt[l], ssm_b_re[l], ssm_b_im[l],
                                     ssm_c_re[l], ssm_c_im[l])
        ys = s5_ssm(proj, batch, seq, 2 * d_conv + 3 * d_attn, bblk, cblk, pw, ssm_d[l], ssm_glu_w[l], ssm_glu_b[l],
                    t['ssm'])
        memkv = norm_matmul(memf, mem_norm_g[l], w_mkv[l].astype(BF16), t['memkv'])
        xf = mix_mem(xf, yc, ya, ys, grp_norm_g[l], w_out[l].astype(BF16), norm_mem_g[l], w_mq[l].astype(BF16), memkv,
                     w_mo[l].astype(BF16), seq, n_mem, t['mix'])

        keys = peer_sub_keys[l].reshape(2 * PEER_HEADS, PEER_KEYS, -1).astype(BF16)
        hb, rank2, e2, cnt, c1 = peer_route(xf, norm_ffn_g[l], peer_wq[l].astype(BF16), keys, t['route'])
        xf = peer_experts(xf, hb, rank2, e2, cnt, c1, u_tabs, vt_tabs, l, final_norm_g, l == depth - 1,
                          t['experts'], t['blocks'])
    return xf.reshape(batch, seq, d)
```
